```python
import math
import jax
import jax.numpy as jnp
from jax import lax
import numpy as np


D_MODEL = 1024
BATCH = 8
SEQ = 2048
DEPTH = 4

GRID_W = 64
CTX_LEN = 256
Q_BLOCK = 128
HEAD_DIM = 64
ROPE_BASE = 10000.0
EPS = 1e-6
D_MIX = D_MODEL
GROUP_WIDTH = D_MIX // 4
D_FF = 4 * D_MODEL

SSD_HEAD_DIM = 64
SSD_HEADS = GROUP_WIDTH // SSD_HEAD_DIM
SSD_INNER = SSD_HEADS * SSD_HEAD_DIM
SSD_GROUPS = 2
SSD_STATE = 64
SSD_CONV = 3
SSD_CHUNK = 128
SSD_CONV_CH = SSD_INNER + 2 * SSD_GROUPS * SSD_STATE
SSD_COLS = SSD_INNER + SSD_CONV_CH + 2 * SSD_HEADS

DIFF_V_DIM = HEAD_DIM
DIFF_HEADS = GROUP_WIDTH // DIFF_V_DIM
DIFF_QK_DIM = DIFF_V_DIM // 2
DIFF_WIDTH = DIFF_HEADS * DIFF_V_DIM
DIFF_COLS = 4 * DIFF_HEADS * DIFF_QK_DIM + DIFF_HEADS * DIFF_V_DIM

GQA_HEADS = GROUP_WIDTH // HEAD_DIM
GQA_KV_HEADS = GQA_HEADS // 2
GQA_COLS = (GQA_HEADS + 2 * GQA_KV_HEADS) * HEAD_DIM

MLA_V_DIM = HEAD_DIM
MLA_HEADS = GROUP_WIDTH // MLA_V_DIM
MLA_NOPE = HEAD_DIM
MLA_ROPE = HEAD_DIM // 2
MLA_Q_LORA = 3 * GROUP_WIDTH // 4
MLA_KV_LORA = GROUP_WIDTH // 2
MLA_COLS = MLA_Q_LORA + MLA_KV_LORA + MLA_ROPE

IN_COLS = SSD_COLS + DIFF_COLS + GQA_COLS + MLA_COLS
COL_SPLITS = (SSD_COLS, SSD_COLS + DIFF_COLS, SSD_COLS + DIFF_COLS + GQA_COLS)

kernel_name = "hybrid_parallel_heads_diffusion_block"


def rmsnorm(x, g):
    xf = x.astype(jnp.float32)
    y = xf * lax.rsqrt(jnp.mean(xf * xf, axis=-1, keepdims=True) + EPS)
    return (y * g.astype(jnp.float32)).astype(x.dtype)


def modulate(h, shift, scale):
    return h * (1.0 + scale) + shift


def squared_relu_mlp(h, w1, w2):
    return jnp.square(jax.nn.relu(h @ w1)) @ w2


def axial_rope(seq, rot_dim):
    rows = seq // GRID_W
    row = jnp.repeat(jnp.arange(rows), GRID_W).astype(jnp.float32)
    col = jnp.tile(jnp.arange(GRID_W), rows).astype(jnp.float32)
    n_freq = rot_dim // 4
    inv = ROPE_BASE ** (-jnp.arange(n_freq, dtype=jnp.float32) / n_freq)
    ang = jnp.concatenate([row[:, None] * inv, col[:, None] * inv], axis=-1)
    return jnp.cos(ang), jnp.sin(ang)


def apply_rope(x, cos, sin):
    half = x.shape[-1] // 2
    x1, x2 = x[..., :half], x[..., half:]
    c = cos.astype(x.dtype)
    s = sin.astype(x.dtype)
    return jnp.concatenate([x1 * c - x2 * s, x1 * s + x2 * c], axis=-1)


def attn_probs(q, k, scale):
    s = jnp.einsum('bkgqd,bktd->bkgqt', q, k).astype(jnp.float32) * scale
    return jax.nn.softmax(s, axis=-1)


def softmax_attention(q, k, v, scale):
    p = attn_probs(q, k, scale).astype(v.dtype)
    return jnp.einsum('bkgqt,bktd->bkgqd', p, v)


def differential_attention(q1, q2, k1, k2, v, lam, scale):
    p = attn_probs(q1, k1, scale) - lam * attn_probs(q2, k2, scale)
    return jnp.einsum('bkgqt,bktd->bkgqd', p.astype(v.dtype), v)


def sweep_query_blocks(fn, *qs):
    b, kh, g, s, _ = qs[0].shape
    nb = s // Q_BLOCK
    blocks = tuple(jnp.moveaxis(q.reshape(b, kh, g, nb, Q_BLOCK, q.shape[-1]), 3, 0) for q in qs)
    out = lax.map(lambda blk: fn(*blk), blocks)
    return jnp.moveaxis(out, 0, 3).reshape(b, kh, g, s, out.shape[-1])


def centred_depthwise_conv(x, w, b):
    k = w.shape[-1]
    rhs = jnp.transpose(w)[:, None, :].astype(x.dtype)
    y = lax.conv_general_dilated(x, rhs, window_strides=(1,), padding=[((k - 1) // 2, k // 2)],
                                 dimension_numbers=('NWC', 'WIO', 'NWC'), feature_group_count=x.shape[-1])
    return y + b.astype(x.dtype)


def seq_flip(t, d):
    return jnp.flip(t, axis=1) if d else t


def segsum_exp(a_cs):
    n = a_cs.shape[-1]
    diff = a_cs[..., :, None] - a_cs[..., None, :]
    mask = jnp.tril(jnp.ones((n, n), dtype=bool))
    return jnp.where(mask, jnp.exp(jnp.where(mask, diff, 0.0)), 0.0)


def ssd_chunked_scan(x, a, bm, cm, h0):
    b, s, h, p = x.shape
    n = bm.shape[-1]
    nc = s // SSD_CHUNK
    xc = x.reshape(b, nc, SSD_CHUNK, h, p)
    bc = bm.reshape(b, nc, SSD_CHUNK, h, n)
    cc = cm.reshape(b, nc, SSD_CHUNK, h, n)
    a_cs = jnp.cumsum(jnp.moveaxis(a.reshape(b, nc, SSD_CHUNK, h), -1, 1), axis=-1)
    scores = jnp.einsum('bclhn,bcshn->bhcls', cc, bc) * segsum_exp(a_cs)
    y_diag = jnp.einsum('bhcls,bcshp->bclhp', scores, xc)
    to_end = jnp.exp(a_cs[..., -1:] - a_cs)
    chunk_states = jnp.einsum('bclhn,bhcl,bclhp->bchpn', bc, to_end, xc)
    chunk_decay = jnp.exp(a_cs[..., -1])

    def step(h_prev, inp):
        st, dec = inp
        return h_prev * dec[..., None, None] + st, h_prev

    h_final, h_starts = lax.scan(step, h0, (jnp.moveaxis(chunk_states, 1, 0), jnp.moveaxis(chunk_decay, -1, 0)))
    h_starts = jnp.moveaxis(h_starts, 0, 1)
    y_off = jnp.einsum('bclhn,bchpn,bhcl->bclhp', cc, h_starts, jnp.exp(a_cs))
    return (y_diag + y_off).reshape(b, s, h, p), h_final


def ssd_mixer(u, uc, conv_w, conv_b, dt_bias, a_log, d_skip, norm_g, with_ctx):
    a_neg = -jnp.exp(a_log.astype(jnp.float32))
    d_skip = d_skip.astype(jnp.float32)

    def prep(v):
        bsz, s = v.shape[:2]
        z, xbc, dt = jnp.split(v, [SSD_INNER, SSD_INNER + SSD_CONV_CH], axis=-1)
        xbc = jax.nn.silu(centred_depthwise_conv(xbc, conv_w, conv_b)).astype(jnp.float32)
        xs, bm, cm = jnp.split(xbc, [SSD_INNER, SSD_INNER + SSD_GROUPS * SSD_STATE], axis=-1)
        rep = SSD_HEADS // SSD_GROUPS
        xs = xs.reshape(bsz, s, SSD_HEADS, SSD_HEAD_DIM)
        bm = jnp.repeat(bm.reshape(bsz, s, SSD_GROUPS, SSD_STATE), rep, axis=2)
        cm = jnp.repeat(cm.reshape(bsz, s, SSD_GROUPS, SSD_STATE), rep, axis=2)
        dt = jax.nn.softplus(dt.astype(jnp.float32).reshape(bsz, s, 2, SSD_HEADS) + dt_bias.astype(jnp.float32))
        return z, xs, bm, cm, dt

    z, xs, bm, cm, dt = prep(u)
    zc, xsc, bmc, cmc, dtc = prep(uc)
    bsz = u.shape[0]
    y = jnp.zeros_like(xs)
    yc = jnp.zeros_like(xsc)
    for d in range(2):
        h0 = jnp.zeros((bsz, SSD_HEADS, SSD_HEAD_DIM, SSD_STATE), jnp.float32)
        ycd, h_ctx = ssd_chunked_scan(seq_flip(xsc * dtc[:, :, d, :, None], d), seq_flip(dtc[:, :, d] * a_neg[d], d),
                                      seq_flip(bmc, d), seq_flip(cmc, d), h0)
        yld, _ = ssd_chunked_scan(seq_flip(xs * dt[:, :, d, :, None], d), seq_flip(dt[:, :, d] * a_neg[d], d),
                                  seq_flip(bm, d), seq_flip(cm, d), h_ctx)
        y = y + seq_flip(yld, d) + d_skip[d][:, None] * xs
        if with_ctx:
            yc = yc + seq_flip(ycd, d) + d_skip[d][:, None] * xsc

    def finish(yy, zz):
        b_, s_ = yy.shape[:2]
        gated = yy.reshape(b_, s_, SSD_INNER) * jax.nn.silu(zz.astype(jnp.float32))
        return rmsnorm(gated, norm_g).astype(zz.dtype)

    return finish(y, z), (finish(yc, zc) if with_ctx else None)


def diff_mixer(u, uc, lam_params, norm_g, lam_init, rope, with_ctx):
    qk_cols = 2 * DIFF_HEADS * DIFF_QK_DIM

    def prep(v, positional, need_q):
        bsz, s = v.shape[:2]
        q, k, val = jnp.split(v, [qk_cols, 2 * qk_cols], axis=-1)
        k = k.reshape(bsz, s, DIFF_HEADS, 2, DIFF_QK_DIM).transpose(3, 0, 2, 1, 4)
        q = q.reshape(bsz, s, DIFF_HEADS, 2, DIFF_QK_DIM).transpose(3, 0, 2, 1, 4) if need_q else None
        if positional:
            q = apply_rope(q, *rope)
            k = apply_rope(k, *rope)
        val = val.reshape(bsz, s, DIFF_HEADS, DIFF_V_DIM).transpose(0, 2, 1, 3)
        return q, k, val

    q, k, v = prep(u, True, True)
    qc, kc, vc = prep(uc, False, with_ctx)
    lp = lam_params.astype(jnp.float32)
    lam = jnp.exp(jnp.sum(lp[0] * lp[1])) - jnp.exp(jnp.sum(lp[2] * lp[3])) + lam_init
    scale = DIFF_QK_DIM ** -0.5
    k_all = jnp.concatenate([kc, k], axis=-2)
    v_all = jnp.concatenate([vc, v], axis=-2)

    def core(q1, q2):
        return differential_attention(q1, q2, k_all[0], k_all[1], v_all, lam, scale)

    def finish(o):
        o = rmsnorm(o, norm_g) * (1.0 - lam_init)
        return o.transpose(0, 2, 1, 3).reshape(o.shape[0], o.shape[2], DIFF_WIDTH)

    y = finish(sweep_query_blocks(core, q[0][:, :, None], q[1][:, :, None])[:, :, 0])
    yc = None
    if with_ctx:
        yc = finish(differential_attention(qc[0][:, :, None], qc[1][:, :, None], kc[0], kc[1], vc, lam, scale)[:, :, 0])
    return y, yc


def gqa_mixer(u, uc, q_norm, k_norm, rope, with_ctx):
    grp = GQA_HEADS // GQA_KV_HEADS

    def prep(v, positional, need_q):
        bsz, s = v.shape[:2]
        q, k, val = jnp.split(v, [GQA_HEADS * HEAD_DIM, (GQA_HEADS + GQA_KV_HEADS) * HEAD_DIM], axis=-1)
        k = rmsnorm(k.reshape(bsz, s, GQA_KV_HEADS, HEAD_DIM), k_norm).transpose(0, 2, 1, 3)
        val = val.reshape(bsz, s, GQA_KV_HEADS, HEAD_DIM).transpose(0, 2, 1, 3)
        q = rmsnorm(q.reshape(bsz, s, GQA_KV_HEADS, grp, HEAD_DIM), q_norm).transpose(0, 2, 3, 1, 4) if need_q else None
        if positional:
            q = apply_rope(q, *rope)
            k = apply_rope(k, *rope)
        return q, k, val

    q, k, v = prep(u, True, True)
    qc, kc, vc = prep(uc, False, with_ctx)
    scale = HEAD_DIM ** -0.5
    k_all = jnp.concatenate([kc, k], axis=-2)
    v_all = jnp.concatenate([vc, v], axis=-2)

    def finish(o):
        return o.transpose(0, 3, 1, 2, 4).reshape(o.shape[0], o.shape[3], GQA_HEADS * HEAD_DIM)

    y = finish(sweep_query_blocks(lambda qb: softmax_attention(qb, k_all, v_all, scale), q))
    yc = finish(softmax_attention(qc, kc, vc, scale)) if with_ctx else None
    return y, yc


def mla_mixer(u, uc, q_norm, kv_norm, w_uq, w_ukv, rope, with_ctx):
    def prep(v, positional, need_q):
        bsz, s = v.shape[:2]
        cq, ckv, k_rope = jnp.split(v, [MLA_Q_LORA, MLA_Q_LORA + MLA_KV_LORA], axis=-1)
        kv = (rmsnorm(ckv, kv_norm) @ w_ukv).reshape(bsz, s, MLA_HEADS, MLA_NOPE + MLA_V_DIM).transpose(0, 2, 1, 3)
        k_nope, val = jnp.split(kv, [MLA_NOPE], axis=-1)
        k_rope = k_rope[:, None]
        q = None
        if need_q:
            q = (rmsnorm(cq, q_norm) @ w_uq).reshape(bsz, s, MLA_HEADS, MLA_NOPE + MLA_ROPE).transpose(0, 2, 1, 3)
            q_nope, q_rope = jnp.split(q, [MLA_NOPE], axis=-1)
            if positional:
                q_rope = apply_rope(q_rope, *rope)
            q = jnp.concatenate([q_nope, q_rope], axis=-1)[:, :, None]
        if positional:
            k_rope = apply_rope(k_rope, *rope)
        k = jnp.concatenate([k_nope, jnp.broadcast_to(k_rope, k_nope.shape[:-1] + (MLA_ROPE,))], axis=-1)
        return q, k, val

    q, k, v = prep(u, True, True)
    qc, kc, vc = prep(uc, False, with_ctx)
    scale = (MLA_NOPE + MLA_ROPE) ** -0.5
    k_all = jnp.concatenate([kc, k], axis=-2)
    v_all = jnp.concatenate([vc, v], axis=-2)

    def finish(o):
        o = o[:, :, 0]
        return o.transpose(0, 2, 1, 3).reshape(o.shape[0], o.shape[2], MLA_HEADS * MLA_V_DIM)

    y = finish(sweep_query_blocks(lambda qb: softmax_attention(qb, k_all, v_all, scale), q))
    yc = finish(softmax_attention(qc, kc, vc, scale)) if with_ctx else None
    return y, yc


def setup_inputs(seed: int = 0) -> dict:
    key = jax.random.key(seed)
    ks = jax.random.split(key, 32)
    f32 = jnp.float32

    def nrm(k, shape, scale):
        return jax.random.normal(k, shape, f32) * scale

    def gain(k, shape):
        return 1.0 + 0.05 * jax.random.normal(k, shape, f32)

    dt0 = jnp.exp(jax.random.uniform(ks[11], (DEPTH, 2, SSD_HEADS), f32, math.log(1e-3), math.log(1e-1)))
    return {
        "x": nrm(ks[0], (BATCH, SEQ, D_MODEL), 1.0),
        "c": nrm(ks[1], (BATCH, D_MODEL), 1.0),
        "ctx": nrm(ks[2], (BATCH, CTX_LEN, D_MODEL), 1.0),
        "c_ctx": nrm(ks[3], (D_MODEL,), 1.0),
        "mod_w": nrm(ks[4], (DEPTH, D_MODEL, 6 * D_MODEL), 0.5 * D_MODEL ** -0.5),
        "mod_b": nrm(ks[5], (DEPTH, 6 * D_MODEL), 0.02),
        "norm1_g": gain(ks[6], (DEPTH, D_MODEL)),
        "norm2_g": gain(ks[7], (DEPTH, D_MODEL)),
        "w_in": nrm(ks[8], (DEPTH, D_MODEL, IN_COLS), D_MODEL ** -0.5),
        "ssd_conv_w": nrm(ks[9], (DEPTH, SSD_CONV_CH, SSD_CONV), SSD_CONV ** -0.5),
        "ssd_conv_b": nrm(ks[10], (DEPTH, SSD_CONV_CH), 0.02),
        "ssd_dt_bias": dt0 + jnp.log(-jnp.expm1(-dt0)),
        "ssd_a_log": jnp.log(jax.random.uniform(ks[12], (DEPTH, 2, SSD_HEADS), f32, 1.0, 16.0)),
        "ssd_d": gain(ks[13], (DEPTH, 2, SSD_HEADS)),
        "ssd_norm_g": gain(ks[14], (DEPTH, SSD_INNER)),
        "diff_lambda": nrm(ks[15], (DEPTH, 4, DIFF_QK_DIM), 0.1),
        "diff_norm_g": gain(ks[16], (DEPTH, DIFF_V_DIM)),
        "gqa_q_norm": gain(ks[17], (DEPTH, HEAD_DIM)),
        "gqa_k_norm": gain(ks[18], (DEPTH, HEAD_DIM)),
        "mla_q_norm": gain(ks[19], (DEPTH, MLA_Q_LORA)),
        "mla_kv_norm": gain(ks[20], (DEPTH, MLA_KV_LORA)),
        "mla_w_uq": nrm(ks[21], (DEPTH, MLA_Q_LORA, MLA_HEADS * (MLA_NOPE + MLA_ROPE)), MLA_Q_LORA ** -0.5),
        "mla_w_ukv": nrm(ks[22], (DEPTH, MLA_KV_LORA, MLA_HEADS * (MLA_NOPE + MLA_V_DIM)), MLA_KV_LORA ** -0.5),
        "w_out": nrm(ks[23], (DEPTH, D_MIX, D_MODEL), D_MIX ** -0.5),
        "mlp_w1": nrm(ks[24], (DEPTH, D_MODEL, D_FF), D_MODEL ** -0.5),
        "mlp_w2": nrm(ks[25], (DEPTH, D_FF, D_MODEL), D_FF ** -0.5),
        "final_norm_g": gain(ks[26], (D_MODEL,)),
    }


def reference(x, c, ctx, c_ctx, mod_w, mod_b, norm1_g, norm2_g, w_in, ssd_conv_w, ssd_conv_b, ssd_dt_bias,
              ssd_a_log, ssd_d, ssd_norm_g, diff_lambda, diff_norm_g, gqa_q_norm, gqa_k_norm, mla_q_norm,
              mla_kv_norm, mla_w_uq, mla_w_ukv, w_out, mlp_w1, mlp_w2, final_norm_g):
    seq = x.shape[1]
    rope_head = axial_rope(seq, HEAD_DIM)
    rope_diff = axial_rope(seq, DIFF_QK_DIM)
    rope_mla = axial_rope(seq, MLA_ROPE)
    cond = jax.nn.silu(c)
    cond_ctx = jax.nn.silu(c_ctx)
    h, hc = x, ctx
    for i in range(DEPTH):
        with_ctx = i < DEPTH - 1
        lam_init = 0.8 - 0.6 * math.exp(-0.3 * i)
        mod = (cond @ mod_w[i] + mod_b[i])[:, None, :]
        mod_c = cond_ctx @ mod_w[i] + mod_b[i]
        sh1, sc1, g1, sh2, sc2, g2 = jnp.split(mod, 6, axis=-1)
        csh1, csc1, cg1, csh2, csc2, cg2 = jnp.split(mod_c, 6, axis=-1)

        u = modulate(rmsnorm(h, norm1_g[i]), sh1, sc1) @ w_in[i]
        uc = modulate(rmsnorm(hc, norm1_g[i]), csh1, csc1) @ w_in[i]
        u_ssd, u_diff, u_gqa, u_mla = jnp.split(u, COL_SPLITS, axis=-1)
        uc_ssd, uc_diff, uc_gqa, uc_mla = jnp.split(uc, COL_SPLITS, axis=-1)

        y_a, yc_a = ssd_mixer(u_ssd, uc_ssd, ssd_conv_w[i], ssd_conv_b[i], ssd_dt_bias[i], ssd_a_log[i],
                              ssd_d[i], ssd_norm_g[i], with_ctx)
        y_b, yc_b = diff_mixer(u_diff, uc_diff, diff_lambda[i], diff_norm_g[i], lam_init, rope_diff, with_ctx)
        y_c, yc_c = gqa_mixer(u_gqa, uc_gqa, gqa_q_norm[i], gqa_k_norm[i], rope_head, with_ctx)
        y_d, yc_d = mla_mixer(u_mla, uc_mla, mla_q_norm[i], mla_kv_norm[i], mla_w_uq[i], mla_w_ukv[i],
                              rope_mla, with_ctx)

        h = h + g1 * (jnp.concatenate([y_a, y_b, y_c, y_d], axis=-1) @ w_out[i])
        h = h + g2 * squared_relu_mlp(modulate(rmsnorm(h, norm2_g[i]), sh2, sc2), mlp_w1[i], mlp_w2[i])
        if with_ctx:
            hc = hc + cg1 * (jnp.concatenate([yc_a, yc_b, yc_c, yc_d], axis=-1) @ w_out[i])
            hc = hc + cg2 * squared_relu_mlp(modulate(rmsnorm(hc, norm2_g[i]), csh2, csc2), mlp_w1[i], mlp_w2[i])
    return rmsnorm(h, final_norm_g)
```

```python
import functools
import math

import jax
import jax.numpy as jnp
import numpy as np
from jax import lax
from jax.experimental import pallas as pl
from jax.experimental.pallas import tpu as pltpu

F32 = jnp.float32
BF16 = jnp.bfloat16

LANES = 128
ROW_TILE = 256
VMEM_LIMIT = 56 * 1024 * 1024

GRID_W = 64
ROPE_BASE = 10000.0
EPS = 1e-6
LOG2E = math.log2(math.e)
NEG_BIG = -1e30

HEAD_DIM = 64
GROUP = 256
SSD_CHUNK = 128
SSD_HEADS = 4
SSD_STATE = 64
SSD_CONV_CH = 512
DIFF_QK = 32
MLA_Q_LORA = 192
MLA_KV_LORA = 128
MLA_ROPE = 32
MLA_NOPE = 64
MLA_HEAD_PAD = 96

U_SSD = 0
U_SSD_W = 896
U_DIFF = U_SSD + U_SSD_W
U_GQA = U_DIFF + 768
U_MLA = U_GQA + 768
U_COLS = U_MLA + 896


def _in_col_sources():
    src = np.full((U_COLS,), -1, np.int64)
    src[0:776] = np.arange(776)
    base = 776
    src[U_DIFF:U_DIFF + 768] = base + np.arange(768)
    base = 776 + 768
    src[U_GQA:U_GQA + 256] = base + np.arange(256)
    dup = np.concatenate([np.arange(64), np.arange(64), 64 + np.arange(64), 64 + np.arange(64)])
    src[U_GQA + 256:U_GQA + 512] = base + 256 + dup
    src[U_GQA + 512:U_GQA + 768] = base + 384 + dup
    base = 776 + 768 + 512
    src[U_MLA:U_MLA + MLA_Q_LORA] = base + np.arange(MLA_Q_LORA)
    src[U_MLA + 256:U_MLA + 384] = base + MLA_Q_LORA + np.arange(MLA_KV_LORA)
    kr = base + MLA_Q_LORA + MLA_KV_LORA + np.arange(MLA_ROPE)
    for h in range(4):
        off = U_MLA + 384 + (h // 2) * 256 + (h % 2) * MLA_HEAD_PAD + MLA_NOPE
        src[off:off + MLA_ROPE] = kr
    return src


def _take_cols(w, src):
    valid = jnp.asarray(src >= 0)
    return jnp.where(valid, jnp.take(w, jnp.asarray(np.maximum(src, 0)), axis=-1), 0.0)


def _mla_uq_sources():
    src = np.full((512,), -1, np.int64)
    src[0:192] = np.arange(192)
    src[256:448] = 192 + np.arange(192)
    return src


def _mla_ukv_sources():
    src = np.full((768,), -1, np.int64)
    for h in range(4):
        off = (h // 2) * 256 + (h % 2) * MLA_HEAD_PAD
        src[off:off + MLA_NOPE] = h * 128 + np.arange(64)
        src[512 + h * 64:512 + (h + 1) * 64] = h * 128 + 64 + np.arange(64)
    return src


def _rope_tables(seq, ctx, rot_dim, lane_slots, width):
    rows = seq // GRID_W
    row = jnp.repeat(jnp.arange(rows), GRID_W).astype(F32)
    col = jnp.tile(jnp.arange(GRID_W), rows).astype(F32)
    n_freq = rot_dim // 4
    inv = ROPE_BASE ** (-jnp.arange(n_freq, dtype=F32) / n_freq)
    ang = jnp.concatenate([row[:, None] * inv, col[:, None] * inv], axis=-1)
    c, s = jnp.cos(ang), jnp.sin(ang)
    cos = jnp.ones((seq, width), F32)
    sin = jnp.zeros((seq, width), F32)
    half = rot_dim // 2
    for off in lane_slots:
        cos = cos.at[:, off:off + half].set(c).at[:, off + half:off + rot_dim].set(c)
        sin = sin.at[:, off:off + half].set(-s).at[:, off + half:off + rot_dim].set(s)
    cos = jnp.concatenate([jnp.ones((ctx, width), F32), cos], axis=0)
    sin = jnp.concatenate([jnp.zeros((ctx, width), F32), sin], axis=0)
    return cos, sin


def _split3(v):
    hi = v.astype(BF16)
    r1 = v - hi.astype(F32)
    mid = r1.astype(BF16)
    lo = (r1 - mid.astype(F32)).astype(BF16)
    return hi, mid, lo


def _dot(a, b):
    return jnp.dot(a, b, preferred_element_type=F32)


def _dot_sel_rhs(v, sel):
    hi, mid, lo = _split3(v)
    return _dot(hi, sel) + _dot(mid, sel) + _dot(lo, sel)


def _dot_sel_lhs(sel, v):
    hi, mid, lo = _split3(v)
    return _dot(sel, hi) + _dot(sel, mid) + _dot(sel, lo)


def _iota(shape, axis):
    return lax.broadcasted_iota(jnp.int32, shape, axis)


def _seg_mean_matrix(width, seg):
    r = _iota((width, width), 0) // seg
    c = _iota((width, width), 1) // seg
    return jnp.where(r == c, 1.0 / seg, 0.0).astype(BF16)


def _rms_rows(x, g, n):
    ms = jnp.sum(x * x, axis=-1, keepdims=True) * (1.0 / n)
    return x * lax.rsqrt(ms + EPS) * g


def _rms_segments(x, g, seg):
    ms = _dot_sel_rhs(x * x, _seg_mean_matrix(x.shape[-1], seg))
    return x * lax.rsqrt(ms + EPS) * g


def _rope(x, cos, sin, rot_dim):
    half = rot_dim // 2
    outs = []
    for s in range(x.shape[-1] // LANES):
        xs = x[:, s * LANES:(s + 1) * LANES]
        up = pltpu.roll(xs, LANES - half, axis=1)
        dn = pltpu.roll(xs, half, axis=1)
        first = (_iota(xs.shape, 1) % rot_dim) < half
        rot = jnp.where(first, up, dn)
        outs.append(xs * cos[:, s * LANES:(s + 1) * LANES] + rot * sin[:, s * LANES:(s + 1) * LANES])
    return jnp.concatenate(outs, axis=-1) if len(outs) > 1 else outs[0]


def _silu(x):
    return x / (1.0 + jnp.exp(-x))


def _softplus(x):
    return jnp.maximum(x, 0.0) + jnp.log(1.0 + jnp.exp(-jnp.abs(x)))


def _mod_kernel(c_ref, w_ref, b_ref, o_ref):
    cond = _silu(c_ref[...]).astype(BF16)
    o_ref[0] = _dot(cond, w_ref[0].astype(BF16)) + b_ref[0]


def _modulation(cond_rows, mod_w, mod_b):
    depth, d, d6 = mod_w.shape
    rows = cond_rows.shape[0]
    return pl.pallas_call(
        _mod_kernel,
        grid=(depth, d6 // d),
        in_specs=[pl.BlockSpec((rows, d), lambda i, j: (0, 0)),
                  pl.BlockSpec((1, d, d), lambda i, j: (i, 0, j)),
                  pl.BlockSpec((1, 1, d), lambda i, j: (i, 0, j))],
        out_specs=pl.BlockSpec((1, rows, d), lambda i, j: (i, 0, j)),
        out_shape=jax.ShapeDtypeStruct((depth, rows, d6), F32),
        compiler_params=pltpu.CompilerParams(dimension_semantics=("parallel", "parallel"),
                                             vmem_limit_bytes=VMEM_LIMIT),
        name="modulation",
    )(cond_rows, mod_w, mod_b.reshape(depth, 1, d6))


def _in_kernel(h_ref, sh_ref, sc_ref, g_ref, w_ref, wuq_ref, wukv_ref, gq_ref, gk_ref, mq_ref, mkv_ref,
               cd_ref, sd_ref, cg_ref, sg_ref, cm_ref, sm_ref,
               ussd_ref, qd_ref, kd_ref, vd_ref, qg_ref, kg_ref, vg_ref, qm_ref, km_ref, vm_ref):
    d_model = h_ref.shape[-1]
    x = h_ref[0]
    xn = _rms_rows(x, g_ref[...], d_model) * (1.0 + sc_ref[0]) + sh_ref[0]
    u = _dot(xn.astype(BF16), w_ref[...])
    ussd_ref[0] = u[:, U_SSD:U_SSD + U_SSD_W]

    q = _rope(u[:, U_DIFF:U_DIFF + 256], cd_ref[...], sd_ref[...], DIFF_QK)
    k = _rope(u[:, U_DIFF + 256:U_DIFF + 512], cd_ref[...], sd_ref[...], DIFF_QK)
    qd_ref[0] = (q * (DIFF_QK ** -0.5 * LOG2E)).astype(BF16)
    kd_ref[0] = k.astype(BF16)
    vd_ref[0] = u[:, U_DIFF + 512:U_DIFF + 768].astype(BF16)

    q = _rms_segments(u[:, U_GQA:U_GQA + 256], gq_ref[...], HEAD_DIM)
    k = _rms_segments(u[:, U_GQA + 256:U_GQA + 512], gk_ref[...], HEAD_DIM)
    q = _rope(q, cg_ref[...], sg_ref[...], HEAD_DIM)
    k = _rope(k, cg_ref[...], sg_ref[...], HEAD_DIM)
    qg_ref[0] = (q * (HEAD_DIM ** -0.5 * LOG2E)).astype(BF16)
    kg_ref[0] = k.astype(BF16)
    vg_ref[0] = u[:, U_GQA + 512:U_GQA + 768].astype(BF16)

    cq = _rms_rows(u[:, U_MLA:U_MLA + 256], mq_ref[...], MLA_Q_LORA)
    q = _rope(_dot(cq.astype(BF16), wuq_ref[...]), cm_ref[...], sm_ref[...], MLA_ROPE)
    qm_ref[0] = (q * ((MLA_NOPE + MLA_ROPE) ** -0.5 * LOG2E)).astype(BF16)
    ckv = _rms_rows(u[:, U_MLA + 256:U_MLA + 384], mkv_ref[...], MLA_KV_LORA)
    kv = _dot(ckv.astype(BF16), wukv_ref[...])
    kr = _rope(u[:, U_MLA + 384:U_MLA + 896], cm_ref[...], sm_ref[...], MLA_ROPE)
    km_ref[0] = (kv[:, 0:512] + kr).astype(BF16)
    vm_ref[0] = kv[:, 512:768].astype(BF16)


def _in_projection(hh, mod3, g1, w_in, w_uq, w_ukv, gq, gk, mq, mkv, tables, *, ctx, batch):
    b, t, d = hh.shape
    nt = t // ROW_TILE
    nctx = ctx // ROW_TILE

    def mod_row(i, bb):
        return jnp.where(i < nctx, batch, bb)

    row_spec = lambda w: pl.BlockSpec((1, ROW_TILE, w), lambda i, bb: (bb, i, 0))
    tab_spec = lambda w: pl.BlockSpec((ROW_TILE, w), lambda i, bb: (i, 0))
    const2 = lambda r, c: pl.BlockSpec((r, c), lambda i, bb: (0, 0))
    in_specs = [
        row_spec(d),
        pl.BlockSpec((1, 1, d), lambda i, bb: (mod_row(i, bb), 0, 0)),
        pl.BlockSpec((1, 1, d), lambda i, bb: (mod_row(i, bb), 0, 1)),
        const2(1, d), const2(d, U_COLS), const2(256, 512), const2(128, 768),
        const2(1, 256), const2(1, 256), const2(1, 256), const2(1, 128),
        tab_spec(256), tab_spec(256), tab_spec(256), tab_spec(256), tab_spec(512), tab_spec(512),
    ]
    widths = [U_SSD_W, 256, 256, 256, 256, 256, 256, 512, 512, 256]
    dtypes = [F32] + [BF16] * 9
    return pl.pallas_call(
        _in_kernel,
        grid=(nt, b),
        in_specs=in_specs,
        out_specs=[row_spec(w) for w in widths],
        out_shape=[jax.ShapeDtypeStruct((b, t, w), dt) for w, dt in zip(widths, dtypes)],
        compiler_params=pltpu.CompilerParams(dimension_semantics=("parallel", "parallel"),
                                             vmem_limit_bytes=VMEM_LIMIT),
        name="in_projection",
    )(hh, mod3, mod3, g1, w_in, w_uq, w_ukv, gq, gk, mq, mkv, *tables)


def _ssd_kernel(u_ref, cw_ref, cb_ref, dtb_ref, alog_ref, dsk_ref, ng_ref, y_ref,
                xbc_s, dt_s, yf_s, yb_s, st_s, *, t_rows, ctx):
    L = SSD_CHUNK
    nc = t_rows // L
    ncc = ctx // L
    row = _iota((L, SSD_CONV_CH), 0)

    def conv_body(c, carry):
        start = pl.multiple_of(c * L, L)
        x = u_ref[0, pl.ds(start, L), 256:768]
        first = jnp.logical_or(c == 0, c == ncc)
        last = jnp.logical_or(c == ncc - 1, c == nc - 1)
        prev8 = u_ref[0, pl.ds(pl.multiple_of(jnp.maximum(start - 8, 0), 8), 8), 256:768]
        next8 = u_ref[0, pl.ds(pl.multiple_of(jnp.minimum(start + L, t_rows - 8), 8), 8), 256:768]
        prev_row = prev8[7:8, :] * jnp.where(first, 0.0, 1.0)
        next_row = next8[0:1, :] * jnp.where(last, 0.0, 1.0)
        xp = jnp.where(row == 0, prev_row, pltpu.roll(x, 1, axis=0))
        xn = jnp.where(row == L - 1, next_row, pltpu.roll(x, L - 1, axis=0))
        y = cw_ref[0:1, :] * xp + cw_ref[1:2, :] * x + cw_ref[2:3, :] * xn + cb_ref[...]
        xbc_s[pl.ds(start, L), :] = _silu(y)
        dt_s[pl.ds(start, L), :] = _softplus(u_ref[0, pl.ds(start, L), 768:896] + dtb_ref[...])
        return carry

    lax.fori_loop(0, nc, conv_body, 0)

    lane = _iota((1, LANES), 1)
    a_neg = jnp.where(lane < 2 * SSD_HEADS, -jnp.exp(alog_ref[...]), 0.0)
    r128 = _iota((L, L), 0)
    c128 = _iota((L, L), 1)
    lane256 = _iota((L, GROUP), 1)
    bd_mask = (_iota((L, GROUP), 0) // SSD_STATE) == (lane256 // (2 * HEAD_DIM))
    st_s[...] = jnp.zeros_like(st_s)

    def process(c, d, y_dst):
        start = pl.multiple_of(c * L, L)
        xs = xbc_s[pl.ds(start, L), 0:256]
        bm = xbc_s[pl.ds(start, L), 256:384]
        cm = xbc_s[pl.ds(start, L), 384:512]
        dtp = dt_s[pl.ds(start, L), :]
        a = dtp * a_neg
        causal = (r128 >= c128) if d == 0 else (r128 <= c128)
        tri = jnp.where(causal, 1.0, 0.0).astype(BF16)
        cs = _dot_sel_lhs(tri, a)
        tot = cs[L - 1:L, :] if d == 0 else cs[0:1, :]
        to_end = jnp.exp(tot - cs)
        in_dec = jnp.exp(cs)
        sel = jnp.where(_iota((LANES, GROUP), 0) == d * SSD_HEADS + _iota((LANES, GROUP), 1) // HEAD_DIM,
                        1.0, 0.0).astype(BF16)
        sel2 = jnp.where(_iota((LANES, 4 * L), 0) == d * SSD_HEADS + _iota((LANES, 4 * L), 1) // L,
                         1.0, 0.0).astype(BF16)
        w_dt = _dot_sel_rhs(dtp, sel)
        w_st = _dot_sel_rhs(dtp * to_end, sel)
        w_in = _dot_sel_rhs(in_dec, sel)
        cs_b = _dot_sel_rhs(cs, sel2)
        cs_t = cs.T
        x_dt = (xs * w_dt).astype(BF16)
        x_st = (xs * w_st).astype(BF16)
        cb16 = cm.astype(BF16)
        bt16 = bm.T.astype(BF16)
        y = jnp.zeros((L, GROUP), F32)
        for g in range(2):
            cg = jnp.where(c128 // SSD_STATE == g, cb16, jnp.zeros_like(cb16))
            gram = _dot(cg, bt16)
            for hh in range(2):
                h = 2 * g + hh
                j = d * SSD_HEADS + h
                delta = cs_b[:, h * L:(h + 1) * L] - cs_t[j:j + 1, :]
                decay = jnp.exp(jnp.where(causal, delta, NEG_BIG))
                yh = _dot((gram * decay).astype(BF16), x_dt)
                y = y + jnp.where(lane256 // HEAD_DIM == h, yh, 0.0)
        state = st_s[d]
        y = y + _dot(cb16, state.astype(BF16)) * w_in
        y_dst[pl.ds(start, L), :] = y
        upd = _dot(bt16, x_st)
        c_dec = w_in[L - 1:L, :] if d == 0 else w_in[0:1, :]
        st_s[d] = state * c_dec + jnp.where(bd_mask, upd, 0.0)

    def scan_body(i, carry):
        process(i, 0, yf_s)
        cb = jnp.where(i < ncc, ncc - 1 - i, nc - 1 - (i - ncc))
        process(cb, 1, yb_s)
        return carry

    lax.fori_loop(0, nc, scan_body, 0)

    d_sum = dsk_ref[0:1, :] + dsk_ref[1:2, :]

    def finish_body(c, carry):
        start = pl.multiple_of(c * L, L)
        xs = xbc_s[pl.ds(start, L), 0:256]
        y = yf_s[pl.ds(start, L), :] + yb_s[pl.ds(start, L), :] + d_sum * xs
        gated = y * _silu(u_ref[0, pl.ds(start, L), 0:256])
        y_ref[0, pl.ds(start, L), :] = _rms_rows(gated, ng_ref[...], GROUP).astype(y_ref.dtype)
        return carry

    lax.fori_loop(0, nc, finish_body, 0)


def _ssd(u_ssd, conv_w, conv_b, dt_bias, a_log, d_skip, norm_g, *, ctx):
    b, t, w = u_ssd.shape
    const2 = lambda r, c: pl.BlockSpec((r, c), lambda bb: (0, 0))
    return pl.pallas_call(
        functools.partial(_ssd_kernel, t_rows=t, ctx=ctx),
        grid=(b,),
        in_specs=[pl.BlockSpec((1, t, w), lambda bb: (bb, 0, 0)),
                  const2(3, SSD_CONV_CH), const2(1, SSD_CONV_CH), const2(1, LANES), const2(1, LANES),
                  const2(2, GROUP), const2(1, GROUP)],
        out_specs=pl.BlockSpec((1, t, GROUP), lambda bb: (bb, 0, 0)),
        out_shape=jax.ShapeDtypeStruct((b, t, GROUP), BF16),
        scratch_shapes=[pltpu.VMEM((t, SSD_CONV_CH), F32), pltpu.VMEM((t, LANES), F32),
                        pltpu.VMEM((t, GROUP), F32), pltpu.VMEM((t, GROUP), F32),
                        pltpu.VMEM((2, SSD_CHUNK, GROUP), F32)],
        compiler_params=pltpu.CompilerParams(dimension_semantics=("parallel",),
                                             vmem_limit_bytes=VMEM_LIMIT),
        name="ssd_mixer",
    )(u_ssd, conv_w, conv_b, dt_bias, a_log, d_skip, norm_g)


def _attn_kernel(*refs, heads, ctx, with_ctx, diff_scale):
    if diff_scale is None:
        q_ref, k_ref, v_ref, o_ref = refs
    else:
        q_ref, k_ref, v_ref, lam_ref, ng_ref, o_ref = refs
    tq = q_ref.shape[1]
    t_rows = k_ref.shape[1]
    lane_q = _iota((tq, GROUP), 1)

    def compute(nk):
        q_all = q_ref[0]
        v = v_ref[0, 0:nk, :]
        if diff_scale is not None:
            lp = lam_ref[...]
            lam = (jnp.exp(jnp.sum(lp[0:1] * lp[1:2], axis=-1, keepdims=True))
                   - jnp.exp(jnp.sum(lp[2:3] * lp[3:4], axis=-1, keepdims=True)) + diff_scale[0])
        acc = jnp.zeros((tq, GROUP), F32)
        for q_block, maps, v0 in heads:
            qb = q_all[:, q_block * GROUP:(q_block + 1) * GROUP]
            kb = k_ref[0, 0:nk, q_block * GROUP:(q_block + 1) * GROUP]
            probs = []
            for lane0, width in maps:
                sel = jnp.logical_and(lane_q >= lane0, lane_q < lane0 + width)
                qm = jnp.where(sel, qb, jnp.zeros_like(qb))
                s = lax.dot_general(qm, kb, (((1,), (1,)), ((), ())), preferred_element_type=F32)
                e = jnp.exp2(s - jnp.max(s, axis=-1, keepdims=True))
                probs.append((e, 1.0 / jnp.sum(e, axis=-1, keepdims=True)))
            out_sel = jnp.logical_and(lane_q >= v0, lane_q < v0 + HEAD_DIM)
            if len(probs) == 1:
                e, inv = probs[0]
                o = _dot(e.astype(BF16), v) * inv
            else:
                (e1, inv1), (e2, inv2) = probs
                o = _dot((e1 * inv1 - e2 * (lam * inv2)).astype(BF16), v)
            acc = acc + jnp.where(out_sel, o, 0.0)
        if diff_scale is not None:
            acc = _rms_segments(acc, ng_ref[...], HEAD_DIM) * diff_scale[1]
        o_ref[0] = acc.astype(o_ref.dtype)

    if with_ctx:
        j = pl.program_id(1)
        n_ctx_blocks = ctx // tq

        @pl.when(j < n_ctx_blocks)
        def _():
            compute(ctx)

        @pl.when(j >= n_ctx_blocks)
        def _():
            compute(t_rows)
    else:
        compute(t_rows)


def _attention(q, k, v, extra, *, heads, ctx, with_ctx, diff_scale, name):
    b, t, dq = q.shape
    off = 0 if with_ctx else ctx // ROW_TILE
    nq = t // ROW_TILE - off
    in_specs = [pl.BlockSpec((1, ROW_TILE, dq), lambda bb, j: (bb, j + off, 0)),
                pl.BlockSpec((1, t, dq), lambda bb, j: (bb, 0, 0)),
                pl.BlockSpec((1, t, GROUP), lambda bb, j: (bb, 0, 0))]
    in_specs += [pl.BlockSpec(e.shape, lambda bb, j: (0, 0)) for e in extra]
    return pl.pallas_call(
        functools.partial(_attn_kernel, heads=heads, ctx=ctx, with_ctx=with_ctx, diff_scale=diff_scale),
        grid=(b, nq),
        in_specs=in_specs,
        out_specs=pl.BlockSpec((1, ROW_TILE, GROUP), lambda bb, j: (bb, j, 0)),
        out_shape=jax.ShapeDtypeStruct((b, nq * ROW_TILE, GROUP), BF16),
        compiler_params=pltpu.CompilerParams(dimension_semantics=("parallel", "parallel"),
                                             vmem_limit_bytes=VMEM_LIMIT),
        name=name,
    )(q, k, v, *extra)


DIFF_HEADS_CFG = tuple((0, ((h * 64, 32), (h * 64 + 32, 32)), h * 64) for h in range(4))
GQA_HEADS_CFG = tuple((0, ((h * 64, 64),), h * 64) for h in range(4))
MLA_HEADS_CFG = tuple((h // 2, (((h % 2) * MLA_HEAD_PAD, MLA_HEAD_PAD),), h * 64) for h in range(4))


def _out_kernel(h_ref, ya_ref, yb_ref, yc_ref, yd_ref, g1_ref, sh2_ref, sc2_ref, g2_ref, n2_ref,
                wo_ref, w1_ref, w2_ref, fg_ref, o_ref, *, final):
    d_model = h_ref.shape[-1]
    ycat = jnp.concatenate([ya_ref[0], yb_ref[0], yc_ref[0], yd_ref[0]], axis=-1)
    h1 = h_ref[0] + g1_ref[0] * _dot(ycat, wo_ref[...])
    t = _rms_rows(h1, n2_ref[...], d_model) * (1.0 + sc2_ref[0]) + sh2_ref[0]
    mid = jnp.maximum(_dot(t.astype(BF16), w1_ref[...]), 0.0)
    h2 = h1 + g2_ref[0] * _dot((mid * mid).astype(BF16), w2_ref[...])
    if final:
        h2 = _rms_rows(h2, fg_ref[...], d_model)
    o_ref[0] = h2


def _out_block(hh, ys, mod3, n2, w_out, w1, w2, fg, *, ctx, batch, final):
    b, t, d = hh.shape
    nctx = ctx // ROW_TILE
    off = nctx if final else 0
    nt = t // ROW_TILE - off
    d_ff = w1.shape[-1]

    def mod_row(i, bb):
        return jnp.where(i + off < nctx, batch, bb)

    row_spec = lambda w: pl.BlockSpec((1, ROW_TILE, w), lambda i, bb: (bb, i + off, 0))
    mod_spec = lambda col: pl.BlockSpec((1, 1, d), lambda i, bb: (mod_row(i, bb), 0, col))
    const2 = lambda r, c: pl.BlockSpec((r, c), lambda i, bb: (0, 0), pipeline_mode=pl.Buffered(1))
    return pl.pallas_call(
        functools.partial(_out_kernel, final=final),
        grid=(nt, b),
        in_specs=[row_spec(d), row_spec(GROUP)]
                 + [pl.BlockSpec((1, ROW_TILE, GROUP), lambda i, bb: (bb, i, 0))] * 3
                 + [mod_spec(2), mod_spec(3), mod_spec(4), mod_spec(5)]
                 + [const2(1, d), const2(d, d), const2(d, d_ff), const2(d_ff, d), const2(1, d)],
        out_specs=pl.BlockSpec((1, ROW_TILE, d), lambda i, bb: (bb, i, 0)),
        out_shape=jax.ShapeDtypeStruct((b, nt * ROW_TILE, d), F32),
        compiler_params=pltpu.CompilerParams(dimension_semantics=("parallel", "parallel"),
                                             vmem_limit_bytes=VMEM_LIMIT),
        name="out_block",
    )(hh, *ys, mod3, mod3, mod3, mod3, n2, w_out, w1, w2, fg)


def kernel(x, c, ctx, c_ctx, mod_w, mod_b, norm1_g, norm2_g, w_in, ssd_conv_w, ssd_conv_b, ssd_dt_bias,
           ssd_a_log, ssd_d, ssd_norm_g, diff_lambda, diff_norm_g, gqa_q_norm, gqa_k_norm, mla_q_norm,
           mla_kv_norm, mla_w_uq, mla_w_ukv, w_out, mlp_w1, mlp_w2, final_norm_g):
    batch, seq, d_model = x.shape
    n_ctx = ctx.shape[1]
    depth = mod_w.shape[0]
    assert n_ctx % ROW_TILE == 0 and seq % ROW_TILE == 0 and seq % GRID_W == 0
    assert w_in.shape[-1] == 2408 and d_model == 4 * GROUP

    pad_rows = -(batch + 1) % 8
    cond_rows = jnp.concatenate([c, c_ctx[None, :], jnp.zeros((pad_rows, d_model), F32)], axis=0)
    mod_all = _modulation(cond_rows, mod_w, mod_b)

    w_in_p = _take_cols(w_in, _in_col_sources()).astype(BF16)
    w_uq_p = jnp.pad(_take_cols(mla_w_uq, _mla_uq_sources()), ((0, 0), (0, 256 - MLA_Q_LORA), (0, 0))).astype(BF16)
    w_ukv_p = _take_cols(mla_w_ukv, _mla_ukv_sources()).astype(BF16)
    w_out_b = w_out.astype(BF16)
    w1_b = mlp_w1.astype(BF16)
    w2_b = mlp_w2.astype(BF16)

    slots32 = [s * 32 for s in range(8)]
    slots64 = [s * 64 for s in range(4)]
    slots_mla = [blk * 256 + hh * MLA_HEAD_PAD + MLA_NOPE for blk in range(2) for hh in range(2)]
    tables = (*_rope_tables(seq, n_ctx, DIFF_QK, slots32, 256),
              *_rope_tables(seq, n_ctx, HEAD_DIM, slots64, 256),
              *_rope_tables(seq, n_ctx, MLA_ROPE, slots_mla, 512))

    tile4 = lambda g: jnp.tile(g, 4)[None, :]
    hh = jnp.concatenate([ctx, x], axis=1)
    for i in range(depth):
        final = i == depth - 1
        with_ctx = not final
        lam_init = 0.8 - 0.6 * math.exp(-0.3 * i)
        mod3 = mod_all[i].reshape(mod_all.shape[1], 1, 6 * d_model)
        mq = jnp.pad(mla_q_norm[i], (0, 256 - MLA_Q_LORA))[None, :]
        (u_ssd, qd, kd, vd, qg, kg, vg, qm, km, vm) = _in_projection(
            hh, mod3, norm1_g[i][None, :], w_in_p[i], w_uq_p[i], w_ukv_p[i],
            tile4(gqa_q_norm[i]), tile4(gqa_k_norm[i]), mq, mla_kv_norm[i][None, :], tables,
            ctx=n_ctx, batch=batch)

        pad8 = lambda v: jnp.pad(v.reshape(-1), (0, LANES - 2 * SSD_HEADS))[None, :]
        y_a = _ssd(u_ssd, ssd_conv_w[i].T, ssd_conv_b[i][None, :], pad8(ssd_dt_bias[i]), pad8(ssd_a_log[i]),
                   jnp.repeat(ssd_d[i], HEAD_DIM, axis=-1), ssd_norm_g[i][None, :], ctx=n_ctx)
        y_b = _attention(qd, kd, vd, (diff_lambda[i], tile4(diff_norm_g[i])), heads=DIFF_HEADS_CFG, ctx=n_ctx,
                         with_ctx=with_ctx, diff_scale=(lam_init, 1.0 - lam_init), name="diff_attention")
        y_c = _attention(qg, kg, vg, (), heads=GQA_HEADS_CFG, ctx=n_ctx, with_ctx=with_ctx,
                         diff_scale=None, name="gqa_attention")
        y_d = _attention(qm, km, vm, (), heads=MLA_HEADS_CFG, ctx=n_ctx, with_ctx=with_ctx,
                         diff_scale=None, name="mla_attention")
        hh = _out_block(hh, (y_a, y_b, y_c, y_d), mod3, norm2_g[i][None, :], w_out_b[i], w1_b[i], w2_b[i],
                        final_norm_g[None, :], ctx=n_ctx, batch=batch, final=final)
    return hh
```

```python
import functools
import math
from typing import NamedTuple

import jax
import jax.numpy as jnp
import numpy as np
from jax import lax
from jax.experimental import pallas as pl
from jax.experimental.pallas import tpu as pltpu

F32 = jnp.float32
BF16 = jnp.bfloat16

LANES = 128
ROW_TILE = 256
ATTN_Q_TILE = 256
VMEM_LIMIT = 56 * 1024 * 1024

GRID_W = 64
ROPE_BASE = 10000.0
EPS = 1e-6
LOG2E = math.log2(math.e)
NEG_BIG = -1e30

HEAD_DIM = 64
GROUP = 256
SSD_CHUNK = 128
SSD_HEADS = 4
SSD_STATE = 64
SSD_CONV_CH = 512
DIFF_QK = 32
MLA_Q_LORA = 192
MLA_KV_LORA = 128
MLA_ROPE = 32
MLA_NOPE = 64
MLA_HEAD_PAD = 96

U_SSD = 0
U_SSD_W = 896
U_DIFF = U_SSD + U_SSD_W
U_GQA = U_DIFF + 768
U_MLA = U_GQA + 768
U_COLS = U_MLA + 896


def _in_col_sources():
    src = np.full((U_COLS,), -1, np.int64)
    src[0:776] = np.arange(776)
    base = 776
    src[U_DIFF:U_DIFF + 768] = base + np.arange(768)
    base = 776 + 768
    src[U_GQA:U_GQA + 256] = base + np.arange(256)
    dup = np.concatenate([np.arange(64), np.arange(64), 64 + np.arange(64), 64 + np.arange(64)])
    src[U_GQA + 256:U_GQA + 512] = base + 256 + dup
    src[U_GQA + 512:U_GQA + 768] = base + 384 + dup
    base = 776 + 768 + 512
    src[U_MLA:U_MLA + MLA_Q_LORA] = base + np.arange(MLA_Q_LORA)
    src[U_MLA + 256:U_MLA + 384] = base + MLA_Q_LORA + np.arange(MLA_KV_LORA)
    kr = base + MLA_Q_LORA + MLA_KV_LORA + np.arange(MLA_ROPE)
    for h in range(4):
        off = U_MLA + 384 + (h // 2) * 256 + (h % 2) * MLA_HEAD_PAD + MLA_NOPE
        src[off:off + MLA_ROPE] = kr
    return src


def _take_cols(w, src):
    parts = []
    start = 0
    for i in range(1, len(src) + 1):
        run_ends = (i == len(src) or (src[i] < 0) != (src[start] < 0)
                    or (src[start] >= 0 and src[i] != src[i - 1] + 1))
        if run_ends:
            if src[start] < 0:
                parts.append(jnp.zeros(w.shape[:-1] + (i - start,), w.dtype))
            else:
                parts.append(w[..., int(src[start]):int(src[start]) + i - start])
            start = i
    return jnp.concatenate(parts, axis=-1)


def _mla_uq_sources():
    src = np.full((512,), -1, np.int64)
    src[0:192] = np.arange(192)
    src[256:448] = 192 + np.arange(192)
    return src


def _mla_ukv_sources():
    src = np.full((768,), -1, np.int64)
    for h in range(4):
        off = (h // 2) * 256 + (h % 2) * MLA_HEAD_PAD
        src[off:off + MLA_NOPE] = h * 128 + np.arange(64)
        src[512 + h * 64:512 + (h + 1) * 64] = h * 128 + 64 + np.arange(64)
    return src


def _rope_tables(seq, ctx, rot_dim, lane_slots, width):
    n_freq = rot_dim // 4
    half = rot_dim // 2
    freq = np.zeros((width,), np.int64)
    use_row = np.zeros((width,), np.float32)
    use_col = np.zeros((width,), np.float32)
    sign = np.zeros((width,), np.float32)
    for off in lane_slots:
        for o in range(rot_dim):
            i = o % half
            freq[off + o] = i % n_freq
            use_row[off + o] = 1.0 if i < n_freq else 0.0
            use_col[off + o] = 0.0 if i < n_freq else 1.0
            sign[off + o] = -1.0 if o < half else 1.0
    pos_row = np.concatenate([np.zeros(ctx), np.repeat(np.arange(seq // GRID_W), GRID_W)]).astype(np.float32)
    pos_col = np.concatenate([np.zeros(ctx), np.tile(np.arange(GRID_W), seq // GRID_W)]).astype(np.float32)
    inv = ROPE_BASE ** (-jnp.arange(n_freq, dtype=F32) / n_freq)
    inv_lane = jnp.take(inv, jnp.asarray(freq))
    ang = (jnp.asarray(pos_row)[:, None] * (inv_lane * use_row)[None, :]
           + jnp.asarray(pos_col)[:, None] * (inv_lane * use_col)[None, :])
    return jnp.cos(ang), jnp.sin(ang) * sign[None, :]


def _split3(v):
    hi = v.astype(BF16)
    r1 = v - hi.astype(F32)
    mid = r1.astype(BF16)
    lo = (r1 - mid.astype(F32)).astype(BF16)
    return hi, mid, lo


def _dot(a, b):
    return jnp.dot(a, b, preferred_element_type=F32)


def _dot_sel_rhs(v, sel):
    hi, mid, lo = _split3(v)
    return _dot(hi, sel) + _dot(mid, sel) + _dot(lo, sel)


def _dot_sel_lhs(sel, v):
    hi, mid, lo = _split3(v)
    return _dot(sel, hi) + _dot(sel, mid) + _dot(sel, lo)


def _iota(shape, axis):
    return lax.broadcasted_iota(jnp.int32, shape, axis)


def _seg_mean_matrix(width, seg):
    r = _iota((width, width), 0) // seg
    c = _iota((width, width), 1) // seg
    return jnp.where(r == c, 1.0 / seg, 0.0).astype(BF16)


def _rms_rows(x, g, n):
    ms = jnp.sum(x * x, axis=-1, keepdims=True) * (1.0 / n)
    return x * lax.rsqrt(ms + EPS) * g


def _rms_segments(x, g, seg):
    ms = _dot_sel_rhs(x * x, _seg_mean_matrix(x.shape[-1], seg))
    return x * lax.rsqrt(ms + EPS) * g


def _rope(x, cos, sin, rot_dim):
    half = rot_dim // 2
    outs = []
    for s in range(x.shape[-1] // LANES):
        xs = x[:, s * LANES:(s + 1) * LANES]
        up = pltpu.roll(xs, LANES - half, axis=1)
        dn = pltpu.roll(xs, half, axis=1)
        first = (_iota(xs.shape, 1) % rot_dim) < half
        rot = jnp.where(first, up, dn)
        outs.append(xs * cos[:, s * LANES:(s + 1) * LANES] + rot * sin[:, s * LANES:(s + 1) * LANES])
    return jnp.concatenate(outs, axis=-1) if len(outs) > 1 else outs[0]


def _silu(x):
    return x / (1.0 + jnp.exp(-x))


def _softplus(x):
    return jnp.maximum(x, 0.0) + jnp.log(1.0 + jnp.exp(-jnp.abs(x)))


def _mod_kernel(c_ref, w_ref, b_ref, o_ref):
    cond = _silu(c_ref[...]).astype(BF16)
    o_ref[0] = _dot(cond, w_ref[0].astype(BF16)) + b_ref[0]


def _modulation(cond_rows, mod_w, mod_b):
    depth, d, d6 = mod_w.shape
    rows = cond_rows.shape[0]
    return pl.pallas_call(
        _mod_kernel,
        grid=(depth, d6 // d),
        in_specs=[pl.BlockSpec((rows, d), lambda i, j: (0, 0)),
                  pl.BlockSpec((1, d, d), lambda i, j: (i, 0, j)),
                  pl.BlockSpec((1, 1, d), lambda i, j: (i, 0, j))],
        out_specs=pl.BlockSpec((1, rows, d), lambda i, j: (i, 0, j)),
        out_shape=jax.ShapeDtypeStruct((depth, rows, d6), F32),
        compiler_params=pltpu.CompilerParams(dimension_semantics=("parallel", "parallel"),
                                             vmem_limit_bytes=VMEM_LIMIT),
        name="modulation",
    )(cond_rows, mod_w, mod_b.reshape(depth, 1, d6))


def _in_kernel(h_ref, sh_ref, sc_ref, g_ref, w_ref, wuq_ref, wukv_ref, gq_ref, gk_ref, mq_ref, mkv_ref,
               cd_ref, sd_ref, cg_ref, sg_ref, cm_ref, sm_ref,
               ussd_ref, qd_ref, kd_ref, vd_ref, qg_ref, kg_ref, vg_ref, qm_ref, km_ref, vm_ref):
    d_model = h_ref.shape[-1]
    x = h_ref[0]
    xn = _rms_rows(x, g_ref[...], d_model) * (1.0 + sc_ref[0]) + sh_ref[0]
    u = _dot(xn.astype(BF16), w_ref[...])
    ussd_ref[0] = u[:, U_SSD:U_SSD + U_SSD_W]

    q = _rope(u[:, U_DIFF:U_DIFF + 256], cd_ref[...], sd_ref[...], DIFF_QK)
    k = _rope(u[:, U_DIFF + 256:U_DIFF + 512], cd_ref[...], sd_ref[...], DIFF_QK)
    qd_ref[0, 0] = (q * (DIFF_QK ** -0.5 * LOG2E)).astype(BF16)
    kd_ref[0, 0] = k.astype(BF16)
    vd_ref[0] = u[:, U_DIFF + 512:U_DIFF + 768].astype(BF16)

    q = _rms_segments(u[:, U_GQA:U_GQA + 256], gq_ref[...], HEAD_DIM)
    k = _rms_segments(u[:, U_GQA + 256:U_GQA + 512], gk_ref[...], HEAD_DIM)
    q = _rope(q, cg_ref[...], sg_ref[...], HEAD_DIM)
    k = _rope(k, cg_ref[...], sg_ref[...], HEAD_DIM)
    qg_ref[0, 0] = (q * (HEAD_DIM ** -0.5 * LOG2E)).astype(BF16)
    kg_ref[0, 0] = k.astype(BF16)
    vg_ref[0] = u[:, U_GQA + 512:U_GQA + 768].astype(BF16)

    cq = _rms_rows(u[:, U_MLA:U_MLA + 256], mq_ref[...], MLA_Q_LORA)
    q = _rope(_dot(cq.astype(BF16), wuq_ref[...]), cm_ref[...], sm_ref[...], MLA_ROPE)
    q = (q * ((MLA_NOPE + MLA_ROPE) ** -0.5 * LOG2E)).astype(BF16)
    qm_ref[0, 0] = q[:, 0:GROUP]
    qm_ref[0, 1] = q[:, GROUP:2 * GROUP]
    ckv = _rms_rows(u[:, U_MLA + 256:U_MLA + 384], mkv_ref[...], MLA_KV_LORA)
    kv = _dot(ckv.astype(BF16), wukv_ref[...])
    kr = _rope(u[:, U_MLA + 384:U_MLA + 896], cm_ref[...], sm_ref[...], MLA_ROPE)
    k = (kv[:, 0:512] + kr).astype(BF16)
    km_ref[0, 0] = k[:, 0:GROUP]
    km_ref[0, 1] = k[:, GROUP:2 * GROUP]
    vm_ref[0] = kv[:, 512:768].astype(BF16)


def _in_projection(hh, mod3, g1, w_in, w_uq, w_ukv, gq, gk, mq, mkv, tables, *, ctx, batch):
    b, t, d = hh.shape
    nt = t // ROW_TILE
    nctx = ctx // ROW_TILE

    def mod_row(i, bb):
        return jnp.where(i < nctx, batch, bb)

    row_spec = lambda w: pl.BlockSpec((1, ROW_TILE, w), lambda i, bb: (bb, i, 0))
    tab_spec = lambda w: pl.BlockSpec((ROW_TILE, w), lambda i, bb: (i, 0))
    const2 = lambda r, c: pl.BlockSpec((r, c), lambda i, bb: (0, 0))
    in_specs = [
        row_spec(d),
        pl.BlockSpec((1, 1, d), lambda i, bb: (mod_row(i, bb), 0, 0)),
        pl.BlockSpec((1, 1, d), lambda i, bb: (mod_row(i, bb), 0, 1)),
        const2(1, d), const2(d, U_COLS), const2(256, 512), const2(128, 768),
        const2(1, 256), const2(1, 256), const2(1, 256), const2(1, 128),
        tab_spec(256), tab_spec(256), tab_spec(256), tab_spec(256), tab_spec(512), tab_spec(512),
    ]
    qk_spec = lambda n: pl.BlockSpec((1, n, ROW_TILE, GROUP), lambda i, bb: (bb, 0, i, 0))
    qk_shape = lambda n: jax.ShapeDtypeStruct((b, n, t, GROUP), BF16)
    v_shape = jax.ShapeDtypeStruct((b, t, GROUP), BF16)
    return pl.pallas_call(
        _in_kernel,
        grid=(nt, b),
        in_specs=in_specs,
        out_specs=[row_spec(U_SSD_W), qk_spec(1), qk_spec(1), row_spec(GROUP), qk_spec(1), qk_spec(1),
                   row_spec(GROUP), qk_spec(2), qk_spec(2), row_spec(GROUP)],
        out_shape=[jax.ShapeDtypeStruct((b, t, U_SSD_W), F32), qk_shape(1), qk_shape(1), v_shape,
                   qk_shape(1), qk_shape(1), v_shape, qk_shape(2), qk_shape(2), v_shape],
        compiler_params=pltpu.CompilerParams(dimension_semantics=("parallel", "parallel"),
                                             vmem_limit_bytes=VMEM_LIMIT),
        name="in_projection",
    )(hh, mod3, mod3, g1, w_in, w_uq, w_ukv, gq, gk, mq, mkv, *tables)


def _ssd_kernel(u_ref, cw_ref, cb_ref, dtb_ref, alog_ref, dsk_ref, ng_ref, y_ref,
                xbc_s, dt_s, yf_s, yb_s, st_s, *, t_rows, ctx):
    L = SSD_CHUNK
    nc = t_rows // L
    ncc = ctx // L
    row = _iota((L, SSD_CONV_CH), 0)

    def conv_body(c, carry):
        start = pl.multiple_of(c * L, L)
        x = u_ref[0, pl.ds(start, L), 256:768]
        first = jnp.logical_or(c == 0, c == ncc)
        last = jnp.logical_or(c == ncc - 1, c == nc - 1)
        prev8 = u_ref[0, pl.ds(pl.multiple_of(jnp.maximum(start - 8, 0), 8), 8), 256:768]
        next8 = u_ref[0, pl.ds(pl.multiple_of(jnp.minimum(start + L, t_rows - 8), 8), 8), 256:768]
        prev_row = prev8[7:8, :] * jnp.where(first, 0.0, 1.0)
        next_row = next8[0:1, :] * jnp.where(last, 0.0, 1.0)
        xp = jnp.where(row == 0, prev_row, pltpu.roll(x, 1, axis=0))
        xn = jnp.where(row == L - 1, next_row, pltpu.roll(x, L - 1, axis=0))
        y = cw_ref[0:1, :] * xp + cw_ref[1:2, :] * x + cw_ref[2:3, :] * xn + cb_ref[...]
        xbc_s[pl.ds(start, L), :] = _silu(y)
        dt_s[pl.ds(start, L), :] = _softplus(u_ref[0, pl.ds(start, L), 768:896] + dtb_ref[...])
        return carry

    lax.fori_loop(0, nc, conv_body, 0)

    lane = _iota((1, LANES), 1)
    a_neg = jnp.where(lane < 2 * SSD_HEADS, -jnp.exp(alog_ref[...]), 0.0)
    r128 = _iota((L, L), 0)
    c128 = _iota((L, L), 1)
    lane256 = _iota((L, GROUP), 1)
    bd_mask = (_iota((L, GROUP), 0) // SSD_STATE) == (lane256 // (2 * HEAD_DIM))
    st_s[...] = jnp.zeros_like(st_s)

    def process(c, d, y_dst):
        start = pl.multiple_of(c * L, L)
        xs = xbc_s[pl.ds(start, L), 0:256]
        bm = xbc_s[pl.ds(start, L), 256:384]
        cm = xbc_s[pl.ds(start, L), 384:512]
        dtp = dt_s[pl.ds(start, L), :]
        a = dtp * a_neg
        causal = (r128 >= c128) if d == 0 else (r128 <= c128)
        tri = jnp.where(causal, 1.0, 0.0).astype(BF16)
        cs = _dot_sel_lhs(tri, a)
        tot = cs[L - 1:L, :] if d == 0 else cs[0:1, :]
        to_end = jnp.exp(tot - cs)
        in_dec = jnp.exp(cs)
        sel = jnp.where(_iota((LANES, GROUP), 0) == d * SSD_HEADS + _iota((LANES, GROUP), 1) // HEAD_DIM,
                        1.0, 0.0).astype(BF16)
        sel2 = jnp.where(_iota((LANES, 4 * L), 0) == d * SSD_HEADS + _iota((LANES, 4 * L), 1) // L,
                         1.0, 0.0).astype(BF16)
        w_dt = _dot_sel_rhs(dtp, sel)
        w_st = _dot_sel_rhs(dtp * to_end, sel)
        w_in = _dot_sel_rhs(in_dec, sel)
        cs_b = _dot_sel_rhs(cs, sel2)
        cs_t = cs.T
        x_dt = (xs * w_dt).astype(BF16)
        x_st = (xs * w_st).astype(BF16)
        cb16 = cm.astype(BF16)
        bt16 = bm.T.astype(BF16)
        y = jnp.zeros((L, GROUP), F32)
        for g in range(2):
            cg = jnp.where(c128 // SSD_STATE == g, cb16, jnp.zeros_like(cb16))
            gram = _dot(cg, bt16)
            for hh in range(2):
                h = 2 * g + hh
                j = d * SSD_HEADS + h
                delta = cs_b[:, h * L:(h + 1) * L] - cs_t[j:j + 1, :]
                decay = jnp.exp(jnp.where(causal, delta, NEG_BIG))
                yh = _dot((gram * decay).astype(BF16), x_dt)
                y = y + jnp.where(lane256 // HEAD_DIM == h, yh, 0.0)
        state = st_s[d]
        y = y + _dot(cb16, state.astype(BF16)) * w_in
        y_dst[pl.ds(start, L), :] = y
        upd = _dot(bt16, x_st)
        c_dec = w_in[L - 1:L, :] if d == 0 else w_in[0:1, :]
        st_s[d] = state * c_dec + jnp.where(bd_mask, upd, 0.0)

    def scan_body(i, carry):
        process(i, 0, yf_s)
        cb = jnp.where(i < ncc, ncc - 1 - i, nc - 1 - (i - ncc))
        process(cb, 1, yb_s)
        return carry

    lax.fori_loop(0, nc, scan_body, 0)

    d_sum = dsk_ref[0:1, :] + dsk_ref[1:2, :]

    def finish_body(c, carry):
        start = pl.multiple_of(c * L, L)
        xs = xbc_s[pl.ds(start, L), 0:256]
        y = yf_s[pl.ds(start, L), :] + yb_s[pl.ds(start, L), :] + d_sum * xs
        gated = y * _silu(u_ref[0, pl.ds(start, L), 0:256])
        y_ref[0, pl.ds(start, L), :] = _rms_rows(gated, ng_ref[...], GROUP).astype(y_ref.dtype)
        return carry

    lax.fori_loop(0, nc, finish_body, 0)


def _ssd(u_ssd, conv_w, conv_b, dt_bias, a_log, d_skip, norm_g, *, ctx):
    b, t, w = u_ssd.shape
    const2 = lambda r, c: pl.BlockSpec((r, c), lambda bb: (0, 0))
    return pl.pallas_call(
        functools.partial(_ssd_kernel, t_rows=t, ctx=ctx),
        grid=(b,),
        in_specs=[pl.BlockSpec((1, t, w), lambda bb: (bb, 0, 0)),
                  const2(3, SSD_CONV_CH), const2(1, SSD_CONV_CH), const2(1, LANES), const2(1, LANES),
                  const2(2, GROUP), const2(1, GROUP)],
        out_specs=pl.BlockSpec((1, t, GROUP), lambda bb: (bb, 0, 0)),
        out_shape=jax.ShapeDtypeStruct((b, t, GROUP), BF16),
        scratch_shapes=[pltpu.VMEM((t, SSD_CONV_CH), F32), pltpu.VMEM((t, LANES), F32),
                        pltpu.VMEM((t, GROUP), F32), pltpu.VMEM((t, GROUP), F32),
                        pltpu.VMEM((2, SSD_CHUNK, GROUP), F32)],
        compiler_params=pltpu.CompilerParams(dimension_semantics=("parallel",),
                                             vmem_limit_bytes=VMEM_LIMIT),
        name="ssd_mixer",
    )(u_ssd, conv_w, conv_b, dt_bias, a_log, d_skip, norm_g)


class AttnCfg(NamedTuple):
    heads: int
    maps: int
    heads_per_block: int
    head_stride: int
    map_width: int


DIFF_CFG = AttnCfg(heads=4, maps=2, heads_per_block=4, head_stride=64, map_width=DIFF_QK)
GQA_CFG = AttnCfg(heads=4, maps=1, heads_per_block=4, head_stride=64, map_width=HEAD_DIM)
MLA_CFG = AttnCfg(heads=4, maps=1, heads_per_block=2, head_stride=MLA_HEAD_PAD, map_width=MLA_HEAD_PAD)


def _attn_kernel(*refs, cfg, ctx, with_ctx, diff_scale, tq):
    if diff_scale is None:
        q_ref, k_ref, v_ref, o_ref, s0, s1, m0, m1 = refs
    else:
        q_ref, k_ref, v_ref, lam_ref, ng_ref, o_ref, s0, s1, m0, m1 = refs
    s_scr, m_scr = (s0, s1), (m0, m1)
    t_rows = k_ref.shape[2]
    out_off = ctx if with_ctx else 0
    n_blocks = (t_rows - ctx) // tq
    lane_q = _iota((tq, GROUP), 1)

    if diff_scale is not None:
        lp = lam_ref[...]
        lam = (jnp.exp(jnp.sum(lp[0:1] * lp[1:2], axis=-1, keepdims=True))
               - jnp.exp(jnp.sum(lp[2:3] * lp[3:4], axis=-1, keepdims=True)) + diff_scale[0])

    def lane_range(lo, width, lanes):
        return jnp.logical_and(lanes >= lo, lanes < lo + width)

    def scores(qb, kb, h):
        out = []
        for j in range(cfg.maps):
            sel = lane_range(h * cfg.head_stride + j * cfg.map_width, cfg.map_width, _iota(qb.shape, 1))
            qm = jnp.where(sel, qb, jnp.zeros_like(qb))
            s = lax.dot_general(qm, kb, (((1,), (1,)), ((), ())), preferred_element_type=F32)
            out.append((s, jnp.max(s, axis=-1, keepdims=True)))
        return out

    def head_output(sm, v):
        probs = []
        for s, m in sm:
            e = jnp.exp2(s - m)
            probs.append((e.astype(BF16), 1.0 / jnp.sum(e, axis=-1, keepdims=True)))
        if cfg.maps == 1:
            e, inv = probs[0]
            return _dot(e, v) * inv
        (e1, inv1), (e2, inv2) = probs
        return _dot(e1 * inv1.astype(BF16) - e2 * (lam * inv2).astype(BF16), v)

    def finish(acc):
        if diff_scale is not None:
            acc = _rms_segments(acc, ng_ref[...], HEAD_DIM) * diff_scale[1]
        return acc.astype(o_ref.dtype)

    assert cfg.heads % 2 == 0

    def issue_scores(qi, h, slot):
        blk, h_in = divmod(h, cfg.heads_per_block)
        row0 = pl.multiple_of(ctx + qi * tq, tq)
        for j, (s, m) in enumerate(scores(q_ref[0, blk, pl.ds(row0, tq), :], k_ref[0, blk], h_in)):
            s_scr[slot][j] = s
            m_scr[slot][j] = m

    def block_body(qi, carry):
        acc = jnp.zeros((tq, GROUP), F32)
        for h in range(cfg.heads):
            if h + 1 < cfg.heads:
                issue_scores(qi, h + 1, (h + 1) % 2)
            else:
                issue_scores(jnp.minimum(qi + 1, n_blocks - 1), 0, 0)
            slot = h % 2
            o = head_output([(s_scr[slot][j], m_scr[slot][j]) for j in range(cfg.maps)], v_ref[0])
            acc = acc + jnp.where(lane_range(h * HEAD_DIM, HEAD_DIM, lane_q), o, 0.0)
        o_ref[0, pl.ds(pl.multiple_of(out_off + qi * tq, tq), tq), :] = finish(acc)
        return carry

    issue_scores(0, 0, 0)
    lax.fori_loop(0, n_blocks, block_body, 0)

    if with_ctx:
        acc = jnp.zeros((ctx, GROUP), F32)
        lane_c = _iota((ctx, GROUP), 1)
        for h in range(cfg.heads):
            blk, h_in = divmod(h, cfg.heads_per_block)
            sm = scores(q_ref[0, blk, 0:ctx, :], k_ref[0, blk, 0:ctx, :], h_in)
            o = head_output(sm, v_ref[0, 0:ctx, :])
            acc = acc + jnp.where(lane_range(h * HEAD_DIM, HEAD_DIM, lane_c), o, 0.0)
        o_ref[0, 0:ctx, :] = finish(acc)


def _attention(q, k, v, extra, *, cfg, ctx, with_ctx, diff_scale, name):
    b, nblk, t, _ = q.shape
    tq = ATTN_Q_TILE
    assert (t - ctx) % tq == 0
    out_rows = t if with_ctx else t - ctx
    qk_spec = pl.BlockSpec((1, nblk, t, GROUP), lambda bb: (bb, 0, 0, 0))
    in_specs = [qk_spec, qk_spec, pl.BlockSpec((1, t, GROUP), lambda bb: (bb, 0, 0))]
    in_specs += [pl.BlockSpec(e.shape, lambda bb: (0, 0)) for e in extra]
    return pl.pallas_call(
        functools.partial(_attn_kernel, cfg=cfg, ctx=ctx, with_ctx=with_ctx, diff_scale=diff_scale, tq=tq),
        grid=(b,),
        in_specs=in_specs,
        out_specs=pl.BlockSpec((1, out_rows, GROUP), lambda bb: (bb, 0, 0)),
        out_shape=jax.ShapeDtypeStruct((b, out_rows, GROUP), BF16),
        scratch_shapes=[pltpu.VMEM((cfg.maps, tq, t), F32)] * 2 + [pltpu.VMEM((cfg.maps, tq, 1), F32)] * 2,
        compiler_params=pltpu.CompilerParams(dimension_semantics=("parallel",),
                                             vmem_limit_bytes=VMEM_LIMIT),
        name=name,
    )(q, k, v, *extra)


def _out_kernel(h_ref, ya_ref, yb_ref, yc_ref, yd_ref, g1_ref, sh2_ref, sc2_ref, g2_ref, n2_ref,
                wo_ref, w1_ref, w2_ref, fg_ref, o_ref, *, final):
    d_model = h_ref.shape[-1]
    ycat = jnp.concatenate([ya_ref[0], yb_ref[0], yc_ref[0], yd_ref[0]], axis=-1)
    h1 = h_ref[0] + g1_ref[0] * _dot(ycat, wo_ref[...])
    t = _rms_rows(h1, n2_ref[...], d_model) * (1.0 + sc2_ref[0]) + sh2_ref[0]
    mid = jnp.maximum(_dot(t.astype(BF16), w1_ref[...]), 0.0)
    h2 = h1 + g2_ref[0] * _dot((mid * mid).astype(BF16), w2_ref[...])
    if final:
        h2 = _rms_rows(h2, fg_ref[...], d_model)
    o_ref[0] = h2


def _out_block(hh, ys, mod3, n2, w_out, w1, w2, fg, *, ctx, batch, final):
    b, t, d = hh.shape
    nctx = ctx // ROW_TILE
    off = nctx if final else 0
    nt = t // ROW_TILE - off
    d_ff = w1.shape[-1]

    def mod_row(i, bb):
        return jnp.where(i + off < nctx, batch, bb)

    row_spec = lambda w: pl.BlockSpec((1, ROW_TILE, w), lambda i, bb: (bb, i + off, 0))
    mod_spec = lambda col: pl.BlockSpec((1, 1, d), lambda i, bb: (mod_row(i, bb), 0, col))
    const2 = lambda r, c: pl.BlockSpec((r, c), lambda i, bb: (0, 0), pipeline_mode=pl.Buffered(1))
    return pl.pallas_call(
        functools.partial(_out_kernel, final=final),
        grid=(nt, b),
        in_specs=[row_spec(d), row_spec(GROUP)]
                 + [pl.BlockSpec((1, ROW_TILE, GROUP), lambda i, bb: (bb, i, 0))] * 3
                 + [mod_spec(2), mod_spec(3), mod_spec(4), mod_spec(5)]
                 + [const2(1, d), const2(d, d), const2(d, d_ff), const2(d_ff, d), const2(1, d)],
        out_specs=pl.BlockSpec((1, ROW_TILE, d), lambda i, bb: (bb, i, 0)),
        out_shape=jax.ShapeDtypeStruct((b, nt * ROW_TILE, d), F32),
        compiler_params=pltpu.CompilerParams(dimension_semantics=("parallel", "parallel"),
                                             vmem_limit_bytes=VMEM_LIMIT),
        name="out_block",
    )(hh, *ys, mod3, mod3, mod3, mod3, n2, w_out, w1, w2, fg)


def kernel(x, c, ctx, c_ctx, mod_w, mod_b, norm1_g, norm2_g, w_in, ssd_conv_w, ssd_conv_b, ssd_dt_bias,
           ssd_a_log, ssd_d, ssd_norm_g, diff_lambda, diff_norm_g, gqa_q_norm, gqa_k_norm, mla_q_norm,
           mla_kv_norm, mla_w_uq, mla_w_ukv, w_out, mlp_w1, mlp_w2, final_norm_g):
    batch, seq, d_model = x.shape
    n_ctx = ctx.shape[1]
    depth = mod_w.shape[0]
    assert n_ctx % ROW_TILE == 0 and seq % ROW_TILE == 0 and seq % GRID_W == 0
    assert w_in.shape[-1] == 2408 and d_model == 4 * GROUP

    pad_rows = -(batch + 1) % 8
    cond_rows = jnp.concatenate([c, c_ctx[None, :], jnp.zeros((pad_rows, d_model), F32)], axis=0)
    mod_all = _modulation(cond_rows, mod_w, mod_b)

    in_src, uq_src, ukv_src = _in_col_sources(), _mla_uq_sources(), _mla_ukv_sources()
    w_in_p = [_take_cols(w_in[i].astype(BF16), in_src) for i in range(depth)]
    w_uq_p = [jnp.pad(_take_cols(mla_w_uq[i].astype(BF16), uq_src), ((0, 256 - MLA_Q_LORA), (0, 0)))
              for i in range(depth)]
    w_ukv_p = [_take_cols(mla_w_ukv[i].astype(BF16), ukv_src) for i in range(depth)]
    w_out_b = [w_out[i].astype(BF16) for i in range(depth)]
    w1_b = [mlp_w1[i].astype(BF16) for i in range(depth)]
    w2_b = [mlp_w2[i].astype(BF16) for i in range(depth)]

    slots32 = [s * 32 for s in range(8)]
    slots64 = [s * 64 for s in range(4)]
    slots_mla = [blk * 256 + hh * MLA_HEAD_PAD + MLA_NOPE for blk in range(2) for hh in range(2)]
    tables = (*_rope_tables(seq, n_ctx, DIFF_QK, slots32, 256),
              *_rope_tables(seq, n_ctx, HEAD_DIM, slots64, 256),
              *_rope_tables(seq, n_ctx, MLA_ROPE, slots_mla, 512))

    tile4 = lambda g: jnp.tile(g, 4)[None, :]
    hh = jnp.concatenate([ctx, x], axis=1)
    for i in range(depth):
        final = i == depth - 1
        with_ctx = not final
        lam_init = 0.8 - 0.6 * math.exp(-0.3 * i)
        mod3 = mod_all[i].reshape(mod_all.shape[1], 1, 6 * d_model)
        mq = jnp.pad(mla_q_norm[i], (0, 256 - MLA_Q_LORA))[None, :]
        (u_ssd, qd, kd, vd, qg, kg, vg, qm, km, vm) = _in_projection(
            hh, mod3, norm1_g[i][None, :], w_in_p[i], w_uq_p[i], w_ukv_p[i],
            tile4(gqa_q_norm[i]), tile4(gqa_k_norm[i]), mq, mla_kv_norm[i][None, :], tables,
            ctx=n_ctx, batch=batch)

        pad8 = lambda v: jnp.pad(v.reshape(-1), (0, LANES - 2 * SSD_HEADS))[None, :]
        y_a = _ssd(u_ssd, ssd_conv_w[i].T, ssd_conv_b[i][None, :], pad8(ssd_dt_bias[i]), pad8(ssd_a_log[i]),
                   jnp.repeat(ssd_d[i], HEAD_DIM, axis=-1), ssd_norm_g[i][None, :], ctx=n_ctx)
        y_b = _attention(qd, kd, vd, (diff_lambda[i], tile4(diff_norm_g[i])), cfg=DIFF_CFG, ctx=n_ctx,
                         with_ctx=with_ctx, diff_scale=(lam_init, 1.0 - lam_init), name="diff_attention")
        y_c = _attention(qg, kg, vg, (), cfg=GQA_CFG, ctx=n_ctx, with_ctx=with_ctx,
                         diff_scale=None, name="gqa_attention")
        y_d = _attention(qm, km, vm, (), cfg=MLA_CFG, ctx=n_ctx, with_ctx=with_ctx,
                         diff_scale=None, name="mla_attention")
        hh = _out_block(hh, (y_a, y_b, y_c, y_d), mod3, norm2_g[i][None, :], w_out_b[i], w1_b[i], w2_b[i],
                        final_norm_g[None, :], ctx=n_ctx, batch=batch, final=final)
    return hh
```

```python
import functools
import math
from typing import NamedTuple

import jax
import jax.numpy as jnp
import numpy as np
from jax import lax
from jax.experimental import pallas as pl
from jax.experimental.pallas import tpu as pltpu

F32 = jnp.float32
BF16 = jnp.bfloat16

LANES = 128
ROW_TILE = 256
ATTN_Q_TILE = 256
VMEM_LIMIT = 56 * 1024 * 1024

GRID_W = 64
ROPE_BASE = 10000.0
EPS = 1e-6
LOG2E = math.log2(math.e)
NEG_BIG = -1e30

HEAD_DIM = 64
GROUP = 256
SSD_CHUNK = 128
SSD_HEADS = 4
SSD_STATE = 64
SSD_CONV_CH = 512
DIFF_QK = 32
MLA_Q_LORA = 192
MLA_KV_LORA = 128
MLA_ROPE = 32
MLA_NOPE = 64
MLA_HEAD_PAD = 96

U_SSD = 0
U_SSD_W = 896
U_DIFF = U_SSD + U_SSD_W
U_GQA = U_DIFF + 768
U_MLA = U_GQA + 768
U_COLS = U_MLA + 640


def _in_col_sources():
    src = np.full((U_COLS,), -1, np.int64)
    src[0:776] = np.arange(776)
    src[776:784] = np.arange(768, 776)
    base = 776
    src[U_DIFF:U_DIFF + 768] = base + np.arange(768)
    base = 776 + 768
    src[U_GQA:U_GQA + 256] = base + np.arange(256)
    dup = np.concatenate([np.arange(64), np.arange(64), 64 + np.arange(64), 64 + np.arange(64)])
    src[U_GQA + 256:U_GQA + 512] = base + 256 + dup
    src[U_GQA + 512:U_GQA + 768] = base + 384 + dup
    base = 776 + 768 + 512
    src[U_MLA:U_MLA + MLA_Q_LORA] = base + np.arange(MLA_Q_LORA)
    src[U_MLA + 256:U_MLA + 384] = base + MLA_Q_LORA + np.arange(MLA_KV_LORA)
    kr = base + MLA_Q_LORA + MLA_KV_LORA + np.arange(MLA_ROPE)
    for h in range(2):
        off = U_MLA + 384 + h * MLA_HEAD_PAD + MLA_NOPE
        src[off:off + MLA_ROPE] = kr
    return src


def _take_cols(w, src):
    parts = []
    start = 0
    for i in range(1, len(src) + 1):
        run_ends = (i == len(src) or (src[i] < 0) != (src[start] < 0)
                    or (src[start] >= 0 and src[i] != src[i - 1] + 1))
        if run_ends:
            if src[start] < 0:
                parts.append(jnp.zeros(w.shape[:-1] + (i - start,), w.dtype))
            else:
                parts.append(w[..., int(src[start]):int(src[start]) + i - start])
            start = i
    return jnp.concatenate(parts, axis=-1)


def _mla_uq_sources():
    src = np.full((512,), -1, np.int64)
    src[0:192] = np.arange(192)
    src[256:448] = 192 + np.arange(192)
    return src


def _mla_ukv_sources():
    src = np.full((768,), -1, np.int64)
    for h in range(4):
        off = (h // 2) * 256 + (h % 2) * MLA_HEAD_PAD
        src[off:off + MLA_NOPE] = h * 128 + np.arange(64)
        src[512 + h * 64:512 + (h + 1) * 64] = h * 128 + 64 + np.arange(64)
    return src


def _rope_tables(seq, ctx, rot_dim, lane_slots, width):
    n_freq = rot_dim // 4
    half = rot_dim // 2
    freq = np.zeros((width,), np.int64)
    use_row = np.zeros((width,), np.float32)
    use_col = np.zeros((width,), np.float32)
    sign = np.zeros((width,), np.float32)
    for off in lane_slots:
        for o in range(rot_dim):
            i = o % half
            freq[off + o] = i % n_freq
            use_row[off + o] = 1.0 if i < n_freq else 0.0
            use_col[off + o] = 0.0 if i < n_freq else 1.0
            sign[off + o] = -1.0 if o < half else 1.0
    pos_row = np.concatenate([np.zeros(ctx), np.repeat(np.arange(seq // GRID_W), GRID_W)]).astype(np.float32)
    pos_col = np.concatenate([np.zeros(ctx), np.tile(np.arange(GRID_W), seq // GRID_W)]).astype(np.float32)
    inv = ROPE_BASE ** (-jnp.arange(n_freq, dtype=F32) / n_freq)
    inv_lane = jnp.take(inv, jnp.asarray(freq))
    ang = (jnp.asarray(pos_row)[:, None] * (inv_lane * use_row)[None, :]
           + jnp.asarray(pos_col)[:, None] * (inv_lane * use_col)[None, :])
    return jnp.cos(ang), jnp.sin(ang) * sign[None, :]


def _split3(v):
    hi = v.astype(BF16)
    r1 = v - hi.astype(F32)
    mid = r1.astype(BF16)
    lo = (r1 - mid.astype(F32)).astype(BF16)
    return hi, mid, lo


def _dot(a, b):
    return jnp.dot(a, b, preferred_element_type=F32)


def _dot_sel_rhs2(v, sel):
    hi = v.astype(BF16)
    lo = (v - hi.astype(F32)).astype(BF16)
    return _dot(hi, sel) + _dot(lo, sel)


def _dot_sel_lhs(sel, v):
    hi, mid, lo = _split3(v)
    return _dot(sel, hi) + _dot(sel, mid) + _dot(sel, lo)


def _iota(shape, axis):
    return lax.broadcasted_iota(jnp.int32, shape, axis)


def _seg_mean_matrix(width, seg):
    r = _iota((width, width), 0) // seg
    c = _iota((width, width), 1) // seg
    return jnp.where(r == c, 1.0 / seg, 0.0).astype(BF16)


def _rms_rows(x, g, n):
    ms = jnp.sum(x * x, axis=-1, keepdims=True) * (1.0 / n)
    return x * lax.rsqrt(ms + EPS) * g


def _rms_segments(x, g, seg):
    ms = _dot_sel_rhs2(x * x, _seg_mean_matrix(x.shape[-1], seg))
    return x * lax.rsqrt(ms + EPS) * g


def _rope(x, cos, sin, rot_dim):
    half = rot_dim // 2
    outs = []
    for s in range(x.shape[-1] // LANES):
        xs = x[:, s * LANES:(s + 1) * LANES]
        up = pltpu.roll(xs, LANES - half, axis=1)
        dn = pltpu.roll(xs, half, axis=1)
        first = (_iota(xs.shape, 1) % rot_dim) < half
        rot = jnp.where(first, up, dn)
        outs.append(xs * cos[:, s * LANES:(s + 1) * LANES] + rot * sin[:, s * LANES:(s + 1) * LANES])
    return jnp.concatenate(outs, axis=-1) if len(outs) > 1 else outs[0]


def _loop_unrolled(n, body, unroll=2):
    if n % unroll:
        unroll = 1

    def group(i, carry):
        for k in range(unroll):
            body(i * unroll + k, carry)
        return carry

    lax.fori_loop(0, n // unroll, group, 0)


def _silu(x):
    return x / (1.0 + jnp.exp(-x))


def _softplus(x):
    return jnp.maximum(x, 0.0) + jnp.log(1.0 + jnp.exp(-jnp.abs(x)))


def _mod_kernel(c_ref, w_ref, b_ref, o_ref):
    cond = _silu(c_ref[...]).astype(BF16)
    o_ref[0] = _dot(cond, w_ref[0].astype(BF16)) + b_ref[0]


def _modulation(cond_rows, mod_w, mod_b):
    depth, d, d6 = mod_w.shape
    rows = cond_rows.shape[0]
    return pl.pallas_call(
        _mod_kernel,
        grid=(depth, d6 // d),
        in_specs=[pl.BlockSpec((rows, d), lambda i, j: (0, 0)),
                  pl.BlockSpec((1, d, d), lambda i, j: (i, 0, j)),
                  pl.BlockSpec((1, 1, d), lambda i, j: (i, 0, j))],
        out_specs=pl.BlockSpec((1, rows, d), lambda i, j: (i, 0, j)),
        out_shape=jax.ShapeDtypeStruct((depth, rows, d6), F32),
        compiler_params=pltpu.CompilerParams(dimension_semantics=("parallel", "parallel"),
                                             vmem_limit_bytes=VMEM_LIMIT),
        name="modulation",
    )(cond_rows, mod_w, mod_b.reshape(depth, 1, d6))


def _in_kernel(h_ref, shb_ref, scb_ref, shc_ref, scc_ref, g_ref, w_ref, wuq_ref, wukv_ref,
               gq_ref, gk_ref, mq_ref, mkv_ref, cd_ref, sd_ref, cg_ref, sg_ref, cm_ref, sm_ref,
               ussd_ref, qd_ref, kd_ref, vd_ref, qg_ref, kg_ref, vg_ref, qm_ref, km_ref, vm_ref, *, n_ctx_tiles):
    d_model = h_ref.shape[-1]
    for k in range(h_ref.shape[1] // ROW_TILE):
        rows = slice(k * ROW_TILE, (k + 1) * ROW_TILE)
        is_ctx = jnp.where(pl.program_id(0) * (h_ref.shape[1] // ROW_TILE) + k < n_ctx_tiles, 1.0, 0.0)
        shift = is_ctx * shc_ref[0] + (1.0 - is_ctx) * shb_ref[0]
        scale = is_ctx * scc_ref[0] + (1.0 - is_ctx) * scb_ref[0]
        xn = _rms_rows(h_ref[0, rows, :], g_ref[...], d_model) * (1.0 + scale) + shift
        u = _dot(xn.astype(BF16), w_ref[...])
        ussd_ref[0, rows, :] = u[:, U_SSD:U_SSD + U_SSD_W]

        q = _rope(u[:, U_DIFF:U_DIFF + 256], cd_ref[rows, :], sd_ref[rows, :], DIFF_QK)
        kk = _rope(u[:, U_DIFF + 256:U_DIFF + 512], cd_ref[rows, :], sd_ref[rows, :], DIFF_QK)
        qd_ref[0, 0, rows, :] = (q * (DIFF_QK ** -0.5 * LOG2E)).astype(BF16)
        kd_ref[0, 0, rows, :] = kk.astype(BF16)
        vd_ref[0, rows, :] = u[:, U_DIFF + 512:U_DIFF + 768].astype(BF16)

        q = _rms_segments(u[:, U_GQA:U_GQA + 256], gq_ref[...], HEAD_DIM)
        kk = _rms_segments(u[:, U_GQA + 256:U_GQA + 512], gk_ref[...], HEAD_DIM)
        q = _rope(q, cg_ref[rows, :], sg_ref[rows, :], HEAD_DIM)
        kk = _rope(kk, cg_ref[rows, :], sg_ref[rows, :], HEAD_DIM)
        qg_ref[0, 0, rows, :] = (q * (HEAD_DIM ** -0.5 * LOG2E)).astype(BF16)
        kg_ref[0, 0, rows, :] = kk.astype(BF16)
        vg_ref[0, rows, :] = u[:, U_GQA + 512:U_GQA + 768].astype(BF16)

        cq = _rms_rows(u[:, U_MLA:U_MLA + 256], mq_ref[...], MLA_Q_LORA)
        q = _rope(_dot(cq.astype(BF16), wuq_ref[...]), cm_ref[rows, :], sm_ref[rows, :], MLA_ROPE)
        q = (q * ((MLA_NOPE + MLA_ROPE) ** -0.5 * LOG2E)).astype(BF16)
        qm_ref[0, 0, rows, :] = q[:, 0:GROUP]
        qm_ref[0, 1, rows, :] = q[:, GROUP:2 * GROUP]
        ckv = _rms_rows(u[:, U_MLA + 256:U_MLA + 384], mkv_ref[...], MLA_KV_LORA)
        kv = _dot(ckv.astype(BF16), wukv_ref[...])
        kr = _rope(u[:, U_MLA + 384:U_MLA + 640], cm_ref[rows, 0:GROUP], sm_ref[rows, 0:GROUP], MLA_ROPE)
        km_ref[0, 0, rows, :] = (kv[:, 0:GROUP] + kr).astype(BF16)
        km_ref[0, 1, rows, :] = (kv[:, GROUP:2 * GROUP] + kr).astype(BF16)
        vm_ref[0, rows, :] = kv[:, 512:768].astype(BF16)


def _in_projection(hh, mod3, g1, w_in, w_uq, w_ukv, gq, gk, mq, mkv, tables, *, ctx, batch):
    b, t, d = hh.shape
    step_rows = next(r for r in (3 * ROW_TILE, 2 * ROW_TILE, ROW_TILE) if t % r == 0)
    nt = t // step_rows

    row_spec = lambda w: pl.BlockSpec((1, step_rows, w), lambda i, bb: (bb, i, 0))
    tab_spec = lambda w: pl.BlockSpec((step_rows, w), lambda i, bb: (i, 0))
    const2 = lambda r, c: pl.BlockSpec((r, c), lambda i, bb: (0, 0))
    in_specs = [
        row_spec(d),
        pl.BlockSpec((1, 1, d), lambda i, bb: (bb, 0, 0)),
        pl.BlockSpec((1, 1, d), lambda i, bb: (bb, 0, 1)),
        pl.BlockSpec((1, 1, d), lambda i, bb: (batch, 0, 0)),
        pl.BlockSpec((1, 1, d), lambda i, bb: (batch, 0, 1)),
        const2(1, d), const2(d, U_COLS), const2(256, 512), const2(128, 768),
        const2(1, 256), const2(1, 256), const2(1, 256), const2(1, 128),
        tab_spec(256), tab_spec(256), tab_spec(256), tab_spec(256), tab_spec(512), tab_spec(512),
    ]
    qk_spec = lambda n: pl.BlockSpec((1, n, step_rows, GROUP), lambda i, bb: (bb, 0, i, 0))
    qk_shape = lambda n: jax.ShapeDtypeStruct((b, n, t, GROUP), BF16)
    v_shape = jax.ShapeDtypeStruct((b, t, GROUP), BF16)
    return pl.pallas_call(
        functools.partial(_in_kernel, n_ctx_tiles=ctx // ROW_TILE),
        grid=(nt, b),
        in_specs=in_specs,
        out_specs=[row_spec(U_SSD_W), qk_spec(1), qk_spec(1), row_spec(GROUP), qk_spec(1), qk_spec(1),
                   row_spec(GROUP), qk_spec(2), qk_spec(2), row_spec(GROUP)],
        out_shape=[jax.ShapeDtypeStruct((b, t, U_SSD_W), F32), qk_shape(1), qk_shape(1), v_shape,
                   qk_shape(1), qk_shape(1), v_shape, qk_shape(2), qk_shape(2), v_shape],
        compiler_params=pltpu.CompilerParams(dimension_semantics=("parallel", "parallel"),
                                             vmem_limit_bytes=VMEM_LIMIT),
        name="in_projection",
    )(hh, mod3, mod3, mod3, mod3, g1, w_in, w_uq, w_ukv, gq, gk, mq, mkv, *tables)


def _ssd_kernel(u_ref, cw_ref, cb_ref, dtb_ref, alog_ref, dsk_ref, ng_ref, y_ref,
                xs_s, c_s, yf_s, yb_s, upd_s, win_s, st_s, *, t_rows, ctx):
    L = SSD_CHUNK
    nc = t_rows // L
    ncc = ctx // L
    n_cols = 2 * SSD_HEADS
    row = _iota((L, SSD_CONV_CH), 0)
    lane = _iota((1, LANES), 1)
    a_neg = jnp.where(lane < n_cols, -jnp.exp(alog_ref[...]), 0.0)
    r128 = _iota((L, L), 0)
    c128 = _iota((L, L), 1)
    lane_l = _iota((L, LANES), 1)
    lane256 = _iota((L, GROUP), 1)
    causal = (r128 >= c128, r128 <= c128)
    tri = jnp.where(causal[0], 1.0, 0.0).astype(BF16)
    bd_mask = (_iota((L, GROUP), 0) // SSD_STATE) == (lane256 // (2 * HEAD_DIM))
    group_lanes = [c128 // SSD_STATE == g for g in range(2)]
    head_lanes = [lane256 // HEAD_DIM == h for h in range(SSD_HEADS)]
    sel = [jnp.where(_iota((LANES, GROUP), 0) == d * SSD_HEADS + _iota((LANES, GROUP), 1) // HEAD_DIM,
                     1.0, 0.0).astype(BF16) for d in range(2)]

    def chunk_body(c, carry):
        start = pl.multiple_of(c * L, L)
        x = u_ref[0, pl.ds(start, L), 256:768]
        first = jnp.logical_or(c == 0, c == ncc)
        last = jnp.logical_or(c == ncc - 1, c == nc - 1)
        prev8 = u_ref[0, pl.ds(pl.multiple_of(jnp.maximum(start - 8, 0), 8), 8), 256:768]
        next8 = u_ref[0, pl.ds(pl.multiple_of(jnp.minimum(start + L, t_rows - 8), 8), 8), 256:768]
        prev_row = prev8[7:8, :] * jnp.where(first, 0.0, 1.0)
        next_row = next8[0:1, :] * jnp.where(last, 0.0, 1.0)
        xp = jnp.where(row == 0, prev_row, pltpu.roll(x, 1, axis=0))
        xn = jnp.where(row == L - 1, next_row, pltpu.roll(x, L - 1, axis=0))
        xbc = _silu(cw_ref[0:1, :] * xp + cw_ref[1:2, :] * x + cw_ref[2:3, :] * xn + cb_ref[...])
        xs, bm, cm = xbc[:, 0:256], xbc[:, 256:384], xbc[:, 384:512]
        dtp = _softplus(u_ref[0, pl.ds(start, L), 768:896] + dtb_ref[...])

        a = dtp * a_neg
        cs_f = _dot_sel_lhs(tri, a)
        tot = cs_f[L - 1:L, :]
        cs = jnp.where(lane_l < SSD_HEADS, cs_f, tot - cs_f + a)
        in_dec = jnp.exp(cs)
        w_end = dtp * jnp.exp(tot - cs)
        rows_t = jnp.where(lane_l < n_cols, cs, dtp).T

        xs16 = xs.astype(BF16)
        cb16 = cm.astype(BF16)
        bt16 = bm.T.astype(BF16)
        gram = [_dot(jnp.where(group_lanes[g], cb16, jnp.zeros_like(cb16)), bt16) for g in range(2)]
        x_heads = jnp.concatenate([jnp.where(head_lanes[h], xs16, jnp.zeros_like(xs16))
                                   for h in range(SSD_HEADS)], axis=0)
        for d, y_dst in ((0, yf_s), (1, yb_s)):
            mixes = []
            for h in range(SSD_HEADS):
                j = d * SSD_HEADS + h
                delta = cs[:, j:j + 1] - rows_t[j:j + 1, :]
                decay = jnp.exp(jnp.where(causal[d], delta, NEG_BIG))
                mixes.append((gram[h // 2] * decay * rows_t[n_cols + j:n_cols + j + 1, :]).astype(BF16))
            y_dst[pl.ds(start, L), :] = _dot(jnp.concatenate(mixes, axis=1), x_heads)
            win_s[d, pl.ds(start, L), :] = _dot_sel_rhs2(in_dec, sel[d])
            x_st = (xs * _dot_sel_rhs2(w_end, sel[d])).astype(BF16)
            upd_s[d, pl.ds(start, L), :] = jnp.where(bd_mask, _dot(bt16, x_st), 0.0)
        xs_s[pl.ds(start, L), :] = xs
        c_s[pl.ds(start, L), :] = cb16
        return carry

    _loop_unrolled(nc, chunk_body)

    st_s[...] = jnp.zeros_like(st_s)

    def carry_state(c, d, y_dst):
        start = pl.multiple_of(c * L, L)
        state = st_s[d]
        w_in = win_s[d, pl.ds(start, L), :]
        y_dst[pl.ds(start, L), :] += _dot(c_s[pl.ds(start, L), :], state.astype(BF16)) * w_in
        c_dec = w_in[L - 1:L, :] if d == 0 else w_in[0:1, :]
        st_s[d] = state * c_dec + upd_s[d, pl.ds(start, L), :]

    def scan_body(i, carry):
        carry_state(i, 0, yf_s)
        carry_state(jnp.where(i < ncc, ncc - 1 - i, nc - 1 - (i - ncc)), 1, yb_s)
        return carry

    _loop_unrolled(nc, scan_body)

    d_sum = dsk_ref[0:1, :] + dsk_ref[1:2, :]

    def finish_body(c, carry):
        start = pl.multiple_of(c * L, L)
        y = yf_s[pl.ds(start, L), :] + yb_s[pl.ds(start, L), :] + d_sum * xs_s[pl.ds(start, L), :]
        gated = y * _silu(u_ref[0, pl.ds(start, L), 0:256])
        y_ref[0, pl.ds(start, L), :] = _rms_rows(gated, ng_ref[...], GROUP).astype(y_ref.dtype)
        return carry

    _loop_unrolled(nc, finish_body)


def _ssd(u_ssd, conv_w, conv_b, dt_bias, a_log, d_skip, norm_g, *, ctx):
    b, t, w = u_ssd.shape
    const2 = lambda r, c: pl.BlockSpec((r, c), lambda bb: (0, 0))
    return pl.pallas_call(
        functools.partial(_ssd_kernel, t_rows=t, ctx=ctx),
        grid=(b,),
        in_specs=[pl.BlockSpec((1, t, w), lambda bb: (bb, 0, 0)),
                  const2(3, SSD_CONV_CH), const2(1, SSD_CONV_CH), const2(1, LANES), const2(1, LANES),
                  const2(2, GROUP), const2(1, GROUP)],
        out_specs=pl.BlockSpec((1, t, GROUP), lambda bb: (bb, 0, 0)),
        out_shape=jax.ShapeDtypeStruct((b, t, GROUP), BF16),
        scratch_shapes=[pltpu.VMEM((t, GROUP), F32), pltpu.VMEM((t, LANES), BF16),
                        pltpu.VMEM((t, GROUP), F32), pltpu.VMEM((t, GROUP), F32),
                        pltpu.VMEM((2, t, GROUP), F32), pltpu.VMEM((2, t, GROUP), F32),
                        pltpu.VMEM((2, SSD_CHUNK, GROUP), F32)],
        compiler_params=pltpu.CompilerParams(dimension_semantics=("parallel",),
                                             vmem_limit_bytes=VMEM_LIMIT),
        name="ssd_mixer",
    )(u_ssd, conv_w, conv_b, dt_bias, a_log, d_skip, norm_g)


class AttnCfg(NamedTuple):
    heads: int
    maps: int
    heads_per_block: int
    head_stride: int
    map_width: int


DIFF_CFG = AttnCfg(heads=4, maps=2, heads_per_block=4, head_stride=64, map_width=DIFF_QK)
GQA_CFG = AttnCfg(heads=4, maps=1, heads_per_block=4, head_stride=64, map_width=HEAD_DIM)
MLA_CFG = AttnCfg(heads=4, maps=1, heads_per_block=2, head_stride=MLA_HEAD_PAD, map_width=MLA_HEAD_PAD)


def _attn_kernel(*refs, cfg, ctx, with_ctx, diff_scale, tq):
    if diff_scale is None:
        q_ref, k_ref, v_ref, o_ref, s0, s1, m0, m1 = refs
    else:
        q_ref, k_ref, v_ref, lam_ref, ng_ref, o_ref, s0, s1, m0, m1 = refs
    s_scr, m_scr = (s0, s1), (m0, m1)
    t_rows = k_ref.shape[2]
    out_off = ctx if with_ctx else 0
    n_blocks = (t_rows - ctx) // tq
    lane_q = _iota((tq, GROUP), 1)

    if diff_scale is not None:
        lp = lam_ref[...]
        lam = (jnp.exp(jnp.sum(lp[0:1] * lp[1:2], axis=-1, keepdims=True))
               - jnp.exp(jnp.sum(lp[2:3] * lp[3:4], axis=-1, keepdims=True)) + diff_scale[0])

    def lane_range(lo, width, lanes):
        return jnp.logical_and(lanes >= lo, lanes < lo + width)

    def scores(qb, kb, h):
        out = []
        for j in range(cfg.maps):
            sel = lane_range(h * cfg.head_stride + j * cfg.map_width, cfg.map_width, _iota(qb.shape, 1))
            qm = jnp.where(sel, qb, jnp.zeros_like(qb))
            s = lax.dot_general(qm, kb, (((1,), (1,)), ((), ())), preferred_element_type=F32)
            out.append((s, jnp.max(s, axis=-1, keepdims=True)))
        return out

    def head_output(sm, v):
        probs = []
        for s, m in sm:
            e = jnp.exp2(s - m)
            probs.append((e.astype(BF16), 1.0 / jnp.sum(e, axis=-1, keepdims=True)))
        if cfg.maps == 1:
            e, inv = probs[0]
            return _dot(e, v) * inv
        (e1, inv1), (e2, inv2) = probs
        return _dot(e1 * inv1.astype(BF16) - e2 * (lam * inv2).astype(BF16), v)

    def finish(acc):
        if diff_scale is not None:
            acc = _rms_segments(acc, ng_ref[...], HEAD_DIM) * diff_scale[1]
        return acc.astype(o_ref.dtype)

    assert cfg.heads % 2 == 0

    def issue_scores(qi, h, slot):
        blk, h_in = divmod(h, cfg.heads_per_block)
        row0 = pl.multiple_of(ctx + qi * tq, tq)
        for j, (s, m) in enumerate(scores(q_ref[0, blk, pl.ds(row0, tq), :], k_ref[0, blk], h_in)):
            s_scr[slot][j] = s
            m_scr[slot][j] = m

    def block_body(qi, carry):
        acc = jnp.zeros((tq, GROUP), F32)
        for h in range(cfg.heads):
            if h + 1 < cfg.heads:
                issue_scores(qi, h + 1, (h + 1) % 2)
            else:
                issue_scores(jnp.minimum(qi + 1, n_blocks - 1), 0, 0)
            slot = h % 2
            o = head_output([(s_scr[slot][j], m_scr[slot][j]) for j in range(cfg.maps)], v_ref[0])
            acc = acc + jnp.where(lane_range(h * HEAD_DIM, HEAD_DIM, lane_q), o, 0.0)
        o_ref[0, pl.ds(pl.multiple_of(out_off + qi * tq, tq), tq), :] = finish(acc)
        return carry

    issue_scores(0, 0, 0)
    lax.fori_loop(0, n_blocks, block_body, 0)

    if with_ctx:
        acc = jnp.zeros((ctx, GROUP), F32)
        lane_c = _iota((ctx, GROUP), 1)
        for h in range(cfg.heads):
            blk, h_in = divmod(h, cfg.heads_per_block)
            sm = scores(q_ref[0, blk, 0:ctx, :], k_ref[0, blk, 0:ctx, :], h_in)
            o = head_output(sm, v_ref[0, 0:ctx, :])
            acc = acc + jnp.where(lane_range(h * HEAD_DIM, HEAD_DIM, lane_c), o, 0.0)
        o_ref[0, 0:ctx, :] = finish(acc)


def _attention(q, k, v, extra, *, cfg, ctx, with_ctx, diff_scale, name):
    b, nblk, t, _ = q.shape
    tq = ATTN_Q_TILE
    assert (t - ctx) % tq == 0
    out_rows = t if with_ctx else t - ctx
    qk_spec = pl.BlockSpec((1, nblk, t, GROUP), lambda bb: (bb, 0, 0, 0))
    in_specs = [qk_spec, qk_spec, pl.BlockSpec((1, t, GROUP), lambda bb: (bb, 0, 0))]
    in_specs += [pl.BlockSpec(e.shape, lambda bb: (0, 0)) for e in extra]
    return pl.pallas_call(
        functools.partial(_attn_kernel, cfg=cfg, ctx=ctx, with_ctx=with_ctx, diff_scale=diff_scale, tq=tq),
        grid=(b,),
        in_specs=in_specs,
        out_specs=pl.BlockSpec((1, out_rows, GROUP), lambda bb: (bb, 0, 0)),
        out_shape=jax.ShapeDtypeStruct((b, out_rows, GROUP), BF16),
        scratch_shapes=[pltpu.VMEM((cfg.maps, tq, t), F32)] * 2 + [pltpu.VMEM((cfg.maps, tq, 1), F32)] * 2,
        compiler_params=pltpu.CompilerParams(dimension_semantics=("parallel",),
                                             vmem_limit_bytes=VMEM_LIMIT),
        name=name,
    )(q, k, v, *extra)


def _out_kernel(h_ref, ya_ref, yb_ref, yc_ref, yd_ref, g1_ref, sh2_ref, sc2_ref, g2_ref, n2_ref,
                wo_ref, w1_ref, w2_ref, fg_ref, o_ref, *, final):
    d_model = h_ref.shape[-1]
    ycat = jnp.concatenate([ya_ref[0], yb_ref[0], yc_ref[0], yd_ref[0]], axis=-1)
    h1 = h_ref[0] + g1_ref[0] * _dot(ycat, wo_ref[...])
    t = _rms_rows(h1, n2_ref[...], d_model) * (1.0 + sc2_ref[0]) + sh2_ref[0]
    mid = jnp.maximum(_dot(t.astype(BF16), w1_ref[...]), 0.0)
    h2 = h1 + g2_ref[0] * _dot((mid * mid).astype(BF16), w2_ref[...])
    if final:
        h2 = _rms_rows(h2, fg_ref[...], d_model)
    o_ref[0] = h2


def _out_block(hh, ys, mod3, n2, w_out, w1, w2, fg, *, ctx, batch, final):
    b, t, d = hh.shape
    nctx = ctx // ROW_TILE
    off = nctx if final else 0
    nt = t // ROW_TILE - off
    d_ff = w1.shape[-1]

    def mod_row(i, bb):
        return jnp.where(i + off < nctx, batch, bb)

    row_spec = lambda w: pl.BlockSpec((1, ROW_TILE, w), lambda i, bb: (bb, i + off, 0))
    mod_spec = lambda col: pl.BlockSpec((1, 1, d), lambda i, bb: (mod_row(i, bb), 0, col))
    const2 = lambda r, c: pl.BlockSpec((r, c), lambda i, bb: (0, 0), pipeline_mode=pl.Buffered(1))
    return pl.pallas_call(
        functools.partial(_out_kernel, final=final),
        grid=(nt, b),
        in_specs=[row_spec(d), row_spec(GROUP)]
                 + [pl.BlockSpec((1, ROW_TILE, GROUP), lambda i, bb: (bb, i, 0))] * 3
                 + [mod_spec(2), mod_spec(3), mod_spec(4), mod_spec(5)]
                 + [const2(1, d), const2(d, d), const2(d, d_ff), const2(d_ff, d), const2(1, d)],
        out_specs=pl.BlockSpec((1, ROW_TILE, d), lambda i, bb: (bb, i, 0)),
        out_shape=jax.ShapeDtypeStruct((b, nt * ROW_TILE, d), F32),
        compiler_params=pltpu.CompilerParams(dimension_semantics=("parallel", "parallel"),
                                             vmem_limit_bytes=VMEM_LIMIT),
        name="out_block",
    )(hh, *ys, mod3, mod3, mod3, mod3, n2, w_out, w1, w2, fg)


def kernel(x, c, ctx, c_ctx, mod_w, mod_b, norm1_g, norm2_g, w_in, ssd_conv_w, ssd_conv_b, ssd_dt_bias,
           ssd_a_log, ssd_d, ssd_norm_g, diff_lambda, diff_norm_g, gqa_q_norm, gqa_k_norm, mla_q_norm,
           mla_kv_norm, mla_w_uq, mla_w_ukv, w_out, mlp_w1, mlp_w2, final_norm_g):
    batch, seq, d_model = x.shape
    n_ctx = ctx.shape[1]
    depth = mod_w.shape[0]
    assert n_ctx % ROW_TILE == 0 and seq % ROW_TILE == 0 and seq % GRID_W == 0
    assert w_in.shape[-1] == 2408 and d_model == 4 * GROUP

    pad_rows = -(batch + 1) % 8
    cond_rows = jnp.concatenate([c, c_ctx[None, :], jnp.zeros((pad_rows, d_model), F32)], axis=0)
    mod_all = _modulation(cond_rows, mod_w, mod_b)

    in_src, uq_src, ukv_src = _in_col_sources(), _mla_uq_sources(), _mla_ukv_sources()
    w_in_p = [_take_cols(w_in[i].astype(BF16), in_src) for i in range(depth)]
    w_uq_p = [jnp.pad(_take_cols(mla_w_uq[i].astype(BF16), uq_src), ((0, 256 - MLA_Q_LORA), (0, 0)))
              for i in range(depth)]
    w_ukv_p = [_take_cols(mla_w_ukv[i].astype(BF16), ukv_src) for i in range(depth)]
    w_out_b = [w_out[i].astype(BF16) for i in range(depth)]
    w1_b = [mlp_w1[i].astype(BF16) for i in range(depth)]
    w2_b = [mlp_w2[i].astype(BF16) for i in range(depth)]

    slots32 = [s * 32 for s in range(8)]
    slots64 = [s * 64 for s in range(4)]
    slots_mla = [blk * 256 + hh * MLA_HEAD_PAD + MLA_NOPE for blk in range(2) for hh in range(2)]
    tables = (*_rope_tables(seq, n_ctx, DIFF_QK, slots32, 256),
              *_rope_tables(seq, n_ctx, HEAD_DIM, slots64, 256),
              *_rope_tables(seq, n_ctx, MLA_ROPE, slots_mla, 512))

    tile4 = lambda g: jnp.tile(g, 4)[None, :]
    hh = jnp.concatenate([ctx, x], axis=1)
    for i in range(depth):
        final = i == depth - 1
        with_ctx = not final
        lam_init = 0.8 - 0.6 * math.exp(-0.3 * i)
        mod3 = mod_all[i].reshape(mod_all.shape[1], 1, 6 * d_model)
        mq = jnp.pad(mla_q_norm[i], (0, 256 - MLA_Q_LORA))[None, :]
        (u_ssd, qd, kd, vd, qg, kg, vg, qm, km, vm) = _in_projection(
            hh, mod3, norm1_g[i][None, :], w_in_p[i], w_uq_p[i], w_ukv_p[i],
            tile4(gqa_q_norm[i]), tile4(gqa_k_norm[i]), mq, mla_kv_norm[i][None, :], tables,
            ctx=n_ctx, batch=batch)

        pad8 = lambda v: jnp.pad(v.reshape(-1), (0, LANES - 2 * SSD_HEADS))[None, :]
        dt_bias2 = jnp.pad(jnp.tile(ssd_dt_bias[i].reshape(-1), 2), (0, LANES - 4 * SSD_HEADS))[None, :]
        y_a = _ssd(u_ssd, ssd_conv_w[i].T, ssd_conv_b[i][None, :], dt_bias2, pad8(ssd_a_log[i]),
                   jnp.repeat(ssd_d[i], HEAD_DIM, axis=-1), ssd_norm_g[i][None, :], ctx=n_ctx)
        y_b = _attention(qd, kd, vd, (diff_lambda[i], tile4(diff_norm_g[i])), cfg=DIFF_CFG, ctx=n_ctx,
                         with_ctx=with_ctx, diff_scale=(lam_init, 1.0 - lam_init), name="diff_attention")
        y_c = _attention(qg, kg, vg, (), cfg=GQA_CFG, ctx=n_ctx, with_ctx=with_ctx,
                         diff_scale=None, name="gqa_attention")
        y_d = _attention(qm, km, vm, (), cfg=MLA_CFG, ctx=n_ctx, with_ctx=with_ctx,
                         diff_scale=None, name="mla_attention")
        hh = _out_block(hh, (y_a, y_b, y_c, y_d), mod3, norm2_g[i][None, :], w_out_b[i], w1_b[i], w2_b[i],
                        final_norm_g[None, :], ctx=n_ctx, batch=batch, final=final)
    return hh
```

```python
import functools
import math
from typing import NamedTuple

import jax
import jax.numpy as jnp
import numpy as np
from jax import lax
from jax.experimental import pallas as pl
from jax.experimental.pallas import tpu as pltpu

F32 = jnp.float32
BF16 = jnp.bfloat16

LANES = 128
ROW_TILE = 256
ATTN_Q_TILE = 256
VMEM_LIMIT = 56 * 1024 * 1024

GRID_W = 64
ROPE_BASE = 10000.0
EPS = 1e-6
LOG2E = math.log2(math.e)
NEG_BIG = -1e30

HEAD_DIM = 64
GROUP = 256
SSD_CHUNK = 128
SSD_HEADS = 4
SSD_STATE = 64
SSD_CONV_CH = 512
DIFF_QK = 32
MLA_Q_LORA = 192
MLA_KV_LORA = 128
MLA_ROPE = 32
MLA_NOPE = 64
MLA_HEAD_PAD = 96

U_SSD = 0
U_SSD_W = 896
U_DIFF = U_SSD + U_SSD_W
U_GQA = U_DIFF + 768
U_MLA = U_GQA + 768
U_COLS = U_MLA + 640


def _in_col_sources():
    src = np.full((U_COLS,), -1, np.int64)
    src[0:776] = np.arange(776)
    src[776:784] = np.arange(768, 776)
    base = 776
    src[U_DIFF:U_DIFF + 768] = base + np.arange(768)
    base = 776 + 768
    src[U_GQA:U_GQA + 256] = base + np.arange(256)
    dup = np.concatenate([np.arange(64), np.arange(64), 64 + np.arange(64), 64 + np.arange(64)])
    src[U_GQA + 256:U_GQA + 512] = base + 256 + dup
    src[U_GQA + 512:U_GQA + 768] = base + 384 + dup
    base = 776 + 768 + 512
    src[U_MLA:U_MLA + MLA_Q_LORA] = base + np.arange(MLA_Q_LORA)
    src[U_MLA + 256:U_MLA + 384] = base + MLA_Q_LORA + np.arange(MLA_KV_LORA)
    kr = base + MLA_Q_LORA + MLA_KV_LORA + np.arange(MLA_ROPE)
    for h in range(2):
        off = U_MLA + 384 + h * MLA_HEAD_PAD + MLA_NOPE
        src[off:off + MLA_ROPE] = kr
    return src


def _take_cols(w, src):
    parts = []
    start = 0
    for i in range(1, len(src) + 1):
        run_ends = (i == len(src) or (src[i] < 0) != (src[start] < 0)
                    or (src[start] >= 0 and src[i] != src[i - 1] + 1))
        if run_ends:
            if src[start] < 0:
                parts.append(jnp.zeros(w.shape[:-1] + (i - start,), w.dtype))
            else:
                parts.append(w[..., int(src[start]):int(src[start]) + i - start])
            start = i
    return jnp.concatenate(parts, axis=-1)


def _mla_uq_sources():
    src = np.full((512,), -1, np.int64)
    src[0:192] = np.arange(192)
    src[256:448] = 192 + np.arange(192)
    return src


def _mla_ukv_sources():
    src = np.full((768,), -1, np.int64)
    for h in range(4):
        off = (h // 2) * 256 + (h % 2) * MLA_HEAD_PAD
        src[off:off + MLA_NOPE] = h * 128 + np.arange(64)
        src[512 + h * 64:512 + (h + 1) * 64] = h * 128 + 64 + np.arange(64)
    return src


def _rope_tables(seq, ctx, rot_dim, lane_slots, width):
    n_freq = rot_dim // 4
    half = rot_dim // 2
    freq = np.zeros((width,), np.int64)
    use_row = np.zeros((width,), np.float32)
    use_col = np.zeros((width,), np.float32)
    sign = np.zeros((width,), np.float32)
    for off in lane_slots:
        for o in range(rot_dim):
            i = o % half
            freq[off + o] = i % n_freq
            use_row[off + o] = 1.0 if i < n_freq else 0.0
            use_col[off + o] = 0.0 if i < n_freq else 1.0
            sign[off + o] = -1.0 if o < half else 1.0
    pos_row = np.concatenate([np.zeros(ctx), np.repeat(np.arange(seq // GRID_W), GRID_W)]).astype(np.float32)
    pos_col = np.concatenate([np.zeros(ctx), np.tile(np.arange(GRID_W), seq // GRID_W)]).astype(np.float32)
    inv = ROPE_BASE ** (-jnp.arange(n_freq, dtype=F32) / n_freq)
    inv_lane = jnp.take(inv, jnp.asarray(freq))
    ang = (jnp.asarray(pos_row)[:, None] * (inv_lane * use_row)[None, :]
           + jnp.asarray(pos_col)[:, None] * (inv_lane * use_col)[None, :])
    return jnp.cos(ang), jnp.sin(ang) * sign[None, :]


def _split3(v):
    hi = v.astype(BF16)
    r1 = v - hi.astype(F32)
    mid = r1.astype(BF16)
    lo = (r1 - mid.astype(F32)).astype(BF16)
    return hi, mid, lo


def _dot(a, b):
    return jnp.dot(a, b, preferred_element_type=F32)


def _dot_sel_rhs2(v, sel):
    hi = v.astype(BF16)
    lo = (v - hi.astype(F32)).astype(BF16)
    return _dot(hi, sel) + _dot(lo, sel)


def _dot_sel_lhs(sel, v):
    hi, mid, lo = _split3(v)
    return _dot(sel, hi) + _dot(sel, mid) + _dot(sel, lo)


def _iota(shape, axis):
    return lax.broadcasted_iota(jnp.int32, shape, axis)


def _seg_mean_matrix(width, seg):
    r = _iota((width, width), 0) // seg
    c = _iota((width, width), 1) // seg
    return jnp.where(r == c, 1.0 / seg, 0.0).astype(BF16)


def _rms_rows(x, g, n):
    ms = jnp.sum(x * x, axis=-1, keepdims=True) * (1.0 / n)
    return x * lax.rsqrt(ms + EPS) * g


def _rms_segments(x, g, seg):
    ms = _dot_sel_rhs2(x * x, _seg_mean_matrix(x.shape[-1], seg))
    return x * lax.rsqrt(ms + EPS) * g


def _rope(x, cos, sin, rot_dim):
    half = rot_dim // 2
    outs = []
    for s in range(x.shape[-1] // LANES):
        xs = x[:, s * LANES:(s + 1) * LANES]
        up = pltpu.roll(xs, LANES - half, axis=1)
        dn = pltpu.roll(xs, half, axis=1)
        first = (_iota(xs.shape, 1) % rot_dim) < half
        rot = jnp.where(first, up, dn)
        outs.append(xs * cos[:, s * LANES:(s + 1) * LANES] + rot * sin[:, s * LANES:(s + 1) * LANES])
    return jnp.concatenate(outs, axis=-1) if len(outs) > 1 else outs[0]


def _loop_unrolled(n, body, unroll=2):
    if n % unroll:
        unroll = 1

    def group(i, carry):
        for k in range(unroll):
            body(i * unroll + k, carry)
        return carry

    lax.fori_loop(0, n // unroll, group, 0)


def _silu(x):
    return x / (1.0 + jnp.exp(-x))


def _softplus(x):
    return jnp.maximum(x, 0.0) + jnp.log(1.0 + jnp.exp(-jnp.abs(x)))


def _mod_kernel(c_ref, w_ref, b_ref, o_ref):
    cond = _silu(c_ref[...]).astype(BF16)
    o_ref[0] = _dot(cond, w_ref[0].astype(BF16)) + b_ref[0]


def _modulation(cond_rows, mod_w, mod_b):
    depth, d, d6 = mod_w.shape
    rows = cond_rows.shape[0]
    return pl.pallas_call(
        _mod_kernel,
        grid=(depth, d6 // d),
        in_specs=[pl.BlockSpec((rows, d), lambda i, j: (0, 0)),
                  pl.BlockSpec((1, d, d), lambda i, j: (i, 0, j)),
                  pl.BlockSpec((1, 1, d), lambda i, j: (i, 0, j))],
        out_specs=pl.BlockSpec((1, rows, d), lambda i, j: (i, 0, j)),
        out_shape=jax.ShapeDtypeStruct((depth, rows, d6), F32),
        compiler_params=pltpu.CompilerParams(dimension_semantics=("parallel", "parallel"),
                                             vmem_limit_bytes=VMEM_LIMIT),
        name="modulation",
    )(cond_rows, mod_w, mod_b.reshape(depth, 1, d6))


def _in_kernel(h_ref, shb_ref, scb_ref, shc_ref, scc_ref, g_ref, w_ref, wuq_ref, wukv_ref,
               gq_ref, gk_ref, mq_ref, mkv_ref, cd_ref, sd_ref, cg_ref, sg_ref, cm_ref, sm_ref,
               ussd_ref, qd_ref, kd_ref, vd_ref, qg_ref, kg_ref, vg_ref, qm_ref, km_ref, vm_ref, *, n_ctx_tiles):
    d_model = h_ref.shape[-1]
    for k in range(h_ref.shape[1] // ROW_TILE):
        rows = slice(k * ROW_TILE, (k + 1) * ROW_TILE)
        is_ctx = jnp.where(pl.program_id(0) * (h_ref.shape[1] // ROW_TILE) + k < n_ctx_tiles, 1.0, 0.0)
        shift = is_ctx * shc_ref[0] + (1.0 - is_ctx) * shb_ref[0]
        scale = is_ctx * scc_ref[0] + (1.0 - is_ctx) * scb_ref[0]
        xn = _rms_rows(h_ref[0, rows, :], g_ref[...], d_model) * (1.0 + scale) + shift
        u = _dot(xn.astype(BF16), w_ref[...])
        ussd_ref[0, rows, :] = u[:, U_SSD:U_SSD + U_SSD_W]

        q = _rope(u[:, U_DIFF:U_DIFF + 256], cd_ref[rows, :], sd_ref[rows, :], DIFF_QK)
        kk = _rope(u[:, U_DIFF + 256:U_DIFF + 512], cd_ref[rows, :], sd_ref[rows, :], DIFF_QK)
        qd_ref[0, 0, rows, :] = (q * (DIFF_QK ** -0.5 * LOG2E)).astype(BF16)
        kd_ref[0, 0, rows, :] = kk.astype(BF16)
        vd_ref[0, rows, :] = u[:, U_DIFF + 512:U_DIFF + 768].astype(BF16)

        q = _rms_segments(u[:, U_GQA:U_GQA + 256], gq_ref[...], HEAD_DIM)
        kk = _rms_segments(u[:, U_GQA + 256:U_GQA + 512], gk_ref[...], HEAD_DIM)
        q = _rope(q, cg_ref[rows, :], sg_ref[rows, :], HEAD_DIM)
        kk = _rope(kk, cg_ref[rows, :], sg_ref[rows, :], HEAD_DIM)
        qg_ref[0, 0, rows, :] = (q * (HEAD_DIM ** -0.5 * LOG2E)).astype(BF16)
        kg_ref[0, 0, rows, :] = kk.astype(BF16)
        vg_ref[0, rows, :] = u[:, U_GQA + 512:U_GQA + 768].astype(BF16)

        cq = _rms_rows(u[:, U_MLA:U_MLA + 256], mq_ref[...], MLA_Q_LORA)
        q = _rope(_dot(cq.astype(BF16), wuq_ref[...]), cm_ref[rows, :], sm_ref[rows, :], MLA_ROPE)
        q = (q * ((MLA_NOPE + MLA_ROPE) ** -0.5 * LOG2E)).astype(BF16)
        qm_ref[0, 0, rows, :] = q[:, 0:GROUP]
        qm_ref[0, 1, rows, :] = q[:, GROUP:2 * GROUP]
        ckv = _rms_rows(u[:, U_MLA + 256:U_MLA + 384], mkv_ref[...], MLA_KV_LORA)
        kv = _dot(ckv.astype(BF16), wukv_ref[...])
        kr = _rope(u[:, U_MLA + 384:U_MLA + 640], cm_ref[rows, 0:GROUP], sm_ref[rows, 0:GROUP], MLA_ROPE)
        km_ref[0, 0, rows, :] = (kv[:, 0:GROUP] + kr).astype(BF16)
        km_ref[0, 1, rows, :] = (kv[:, GROUP:2 * GROUP] + kr).astype(BF16)
        vm_ref[0, rows, :] = kv[:, 512:768].astype(BF16)


def _in_projection(hh, mod3, g1, w_in, w_uq, w_ukv, gq, gk, mq, mkv, tables, *, layer, ctx, batch):
    b, t, d = hh.shape
    step_rows = next(r for r in (3 * ROW_TILE, 2 * ROW_TILE, ROW_TILE) if t % r == 0)
    nt = t // step_rows

    row_spec = lambda w: pl.BlockSpec((1, step_rows, w), lambda i, bb: (bb, i, 0))
    tab_spec = lambda w: pl.BlockSpec((step_rows, w), lambda i, bb: (i, 0))
    const2 = lambda r, c: pl.BlockSpec((r, c), lambda i, bb: (0, 0))
    layer_w = lambda r, c: pl.BlockSpec((None, r, c), lambda i, bb: (layer, 0, 0))
    in_specs = [
        row_spec(d),
        pl.BlockSpec((1, 1, d), lambda i, bb: (bb, 0, 0)),
        pl.BlockSpec((1, 1, d), lambda i, bb: (bb, 0, 1)),
        pl.BlockSpec((1, 1, d), lambda i, bb: (batch, 0, 0)),
        pl.BlockSpec((1, 1, d), lambda i, bb: (batch, 0, 1)),
        const2(1, d), layer_w(d, U_COLS), layer_w(256, 512), layer_w(128, 768),
        const2(1, 256), const2(1, 256), const2(1, 256), const2(1, 128),
        tab_spec(256), tab_spec(256), tab_spec(256), tab_spec(256), tab_spec(512), tab_spec(512),
    ]
    qk_spec = lambda n: pl.BlockSpec((1, n, step_rows, GROUP), lambda i, bb: (bb, 0, i, 0))
    qk_shape = lambda n: jax.ShapeDtypeStruct((b, n, t, GROUP), BF16)
    v_shape = jax.ShapeDtypeStruct((b, t, GROUP), BF16)
    return pl.pallas_call(
        functools.partial(_in_kernel, n_ctx_tiles=ctx // ROW_TILE),
        grid=(nt, b),
        in_specs=in_specs,
        out_specs=[row_spec(U_SSD_W), qk_spec(1), qk_spec(1), row_spec(GROUP), qk_spec(1), qk_spec(1),
                   row_spec(GROUP), qk_spec(2), qk_spec(2), row_spec(GROUP)],
        out_shape=[jax.ShapeDtypeStruct((b, t, U_SSD_W), F32), qk_shape(1), qk_shape(1), v_shape,
                   qk_shape(1), qk_shape(1), v_shape, qk_shape(2), qk_shape(2), v_shape],
        compiler_params=pltpu.CompilerParams(dimension_semantics=("parallel", "parallel"),
                                             vmem_limit_bytes=VMEM_LIMIT),
        name="in_projection",
    )(hh, mod3, mod3, mod3, mod3, g1, w_in, w_uq, w_ukv, gq, gk, mq, mkv, *tables)


def _ssd_kernel(u_ref, cw_ref, cb_ref, dtb_ref, alog_ref, dsk_ref, ng_ref, y_ref,
                xs_s, c_s, yf_s, yb_s, upd_s, win_s, st_s, *, t_rows, ctx):
    L = SSD_CHUNK
    nc = t_rows // L
    ncc = ctx // L
    n_cols = 2 * SSD_HEADS
    row = _iota((L, SSD_CONV_CH), 0)
    lane = _iota((1, LANES), 1)
    a_neg = jnp.where(lane < n_cols, -jnp.exp(alog_ref[...]), 0.0)
    r128 = _iota((L, L), 0)
    c128 = _iota((L, L), 1)
    lane_l = _iota((L, LANES), 1)
    lane256 = _iota((L, GROUP), 1)
    causal = (r128 >= c128, r128 <= c128)
    tri = jnp.where(causal[0], 1.0, 0.0).astype(BF16)
    bd_mask = (_iota((L, GROUP), 0) // SSD_STATE) == (lane256 // (2 * HEAD_DIM))
    group_lanes = [c128 // SSD_STATE == g for g in range(2)]
    head_lanes = [lane256 // HEAD_DIM == h for h in range(SSD_HEADS)]
    sel = [jnp.where(_iota((LANES, GROUP), 0) == d * SSD_HEADS + _iota((LANES, GROUP), 1) // HEAD_DIM,
                     1.0, 0.0).astype(BF16) for d in range(2)]

    def chunk_body(c, carry):
        start = pl.multiple_of(c * L, L)
        x = u_ref[0, pl.ds(start, L), 256:768]
        first = jnp.logical_or(c == 0, c == ncc)
        last = jnp.logical_or(c == ncc - 1, c == nc - 1)
        prev8 = u_ref[0, pl.ds(pl.multiple_of(jnp.maximum(start - 8, 0), 8), 8), 256:768]
        next8 = u_ref[0, pl.ds(pl.multiple_of(jnp.minimum(start + L, t_rows - 8), 8), 8), 256:768]
        prev_row = prev8[7:8, :] * jnp.where(first, 0.0, 1.0)
        next_row = next8[0:1, :] * jnp.where(last, 0.0, 1.0)
        xp = jnp.where(row == 0, prev_row, pltpu.roll(x, 1, axis=0))
        xn = jnp.where(row == L - 1, next_row, pltpu.roll(x, L - 1, axis=0))
        xbc = _silu(cw_ref[0:1, :] * xp + cw_ref[1:2, :] * x + cw_ref[2:3, :] * xn + cb_ref[...])
        xs, bm, cm = xbc[:, 0:256], xbc[:, 256:384], xbc[:, 384:512]
        dtp = _softplus(u_ref[0, pl.ds(start, L), 768:896] + dtb_ref[...])

        a = dtp * a_neg
        cs_f = _dot_sel_lhs(tri, a)
        tot = cs_f[L - 1:L, :]
        cs = jnp.where(lane_l < SSD_HEADS, cs_f, tot - cs_f + a)
        in_dec = jnp.exp(cs)
        w_end = dtp * jnp.exp(tot - cs)
        rows_t = jnp.where(lane_l < n_cols, cs, dtp).T

        xs16 = xs.astype(BF16)
        cb16 = cm.astype(BF16)
        bt16 = bm.T.astype(BF16)
        gram = [_dot(jnp.where(group_lanes[g], cb16, jnp.zeros_like(cb16)), bt16) for g in range(2)]
        x_heads = jnp.concatenate([jnp.where(head_lanes[h], xs16, jnp.zeros_like(xs16))
                                   for h in range(SSD_HEADS)], axis=0)
        for d, y_dst in ((0, yf_s), (1, yb_s)):
            mixes = []
            for h in range(SSD_HEADS):
                j = d * SSD_HEADS + h
                delta = cs[:, j:j + 1] - rows_t[j:j + 1, :]
                decay = jnp.exp(jnp.where(causal[d], delta, NEG_BIG))
                mixes.append((gram[h // 2] * decay * rows_t[n_cols + j:n_cols + j + 1, :]).astype(BF16))
            y_dst[pl.ds(start, L), :] = _dot(jnp.concatenate(mixes, axis=1), x_heads)
            win_s[d, pl.ds(start, L), :] = _dot_sel_rhs2(in_dec, sel[d])
            x_st = (xs * _dot_sel_rhs2(w_end, sel[d])).astype(BF16)
            upd_s[d, pl.ds(start, L), :] = jnp.where(bd_mask, _dot(bt16, x_st), 0.0)
        xs_s[pl.ds(start, L), :] = xs
        c_s[pl.ds(start, L), :] = cb16
        return carry

    _loop_unrolled(nc, chunk_body)

    st_s[...] = jnp.zeros_like(st_s)

    def carry_state(c, d, y_dst):
        start = pl.multiple_of(c * L, L)
        state = st_s[d]
        w_in = win_s[d, pl.ds(start, L), :]
        y_dst[pl.ds(start, L), :] += _dot(c_s[pl.ds(start, L), :], state.astype(BF16)) * w_in
        c_dec = w_in[L - 1:L, :] if d == 0 else w_in[0:1, :]
        st_s[d] = state * c_dec + upd_s[d, pl.ds(start, L), :]

    def scan_body(i, carry):
        carry_state(i, 0, yf_s)
        carry_state(jnp.where(i < ncc, ncc - 1 - i, nc - 1 - (i - ncc)), 1, yb_s)
        return carry

    _loop_unrolled(nc, scan_body)

    d_sum = dsk_ref[0:1, :] + dsk_ref[1:2, :]

    def finish_body(c, carry):
        start = pl.multiple_of(c * L, L)
        y = yf_s[pl.ds(start, L), :] + yb_s[pl.ds(start, L), :] + d_sum * xs_s[pl.ds(start, L), :]
        gated = y * _silu(u_ref[0, pl.ds(start, L), 0:256])
        y_ref[0, pl.ds(start, L), :] = _rms_rows(gated, ng_ref[...], GROUP).astype(y_ref.dtype)
        return carry

    _loop_unrolled(nc, finish_body)


def _ssd(u_ssd, conv_w, conv_b, dt_bias, a_log, d_skip, norm_g, *, ctx):
    b, t, w = u_ssd.shape
    const2 = lambda r, c: pl.BlockSpec((r, c), lambda bb: (0, 0))
    return pl.pallas_call(
        functools.partial(_ssd_kernel, t_rows=t, ctx=ctx),
        grid=(b,),
        in_specs=[pl.BlockSpec((1, t, w), lambda bb: (bb, 0, 0)),
                  const2(3, SSD_CONV_CH), const2(1, SSD_CONV_CH), const2(1, LANES), const2(1, LANES),
                  const2(2, GROUP), const2(1, GROUP)],
        out_specs=pl.BlockSpec((1, t, GROUP), lambda bb: (bb, 0, 0)),
        out_shape=jax.ShapeDtypeStruct((b, t, GROUP), BF16),
        scratch_shapes=[pltpu.VMEM((t, GROUP), F32), pltpu.VMEM((t, LANES), BF16),
                        pltpu.VMEM((t, GROUP), F32), pltpu.VMEM((t, GROUP), F32),
                        pltpu.VMEM((2, t, GROUP), F32), pltpu.VMEM((2, t, GROUP), F32),
                        pltpu.VMEM((2, SSD_CHUNK, GROUP), F32)],
        compiler_params=pltpu.CompilerParams(dimension_semantics=("parallel",),
                                             vmem_limit_bytes=VMEM_LIMIT),
        name="ssd_mixer",
    )(u_ssd, conv_w, conv_b, dt_bias, a_log, d_skip, norm_g)


class AttnCfg(NamedTuple):
    heads: int
    maps: int
    heads_per_block: int
    head_stride: int
    map_width: int


DIFF_CFG = AttnCfg(heads=4, maps=2, heads_per_block=4, head_stride=64, map_width=DIFF_QK)
GQA_CFG = AttnCfg(heads=4, maps=1, heads_per_block=4, head_stride=64, map_width=HEAD_DIM)
MLA_CFG = AttnCfg(heads=4, maps=1, heads_per_block=2, head_stride=MLA_HEAD_PAD, map_width=MLA_HEAD_PAD)


class _AttnGroup:
    def __init__(self, cfg, diff_scale, refs_in, o_ref, scratch, ctx, with_ctx, tq):
        self.cfg, self.diff_scale, self.ctx, self.tq = cfg, diff_scale, ctx, tq
        self.q_ref, self.k_ref, self.v_ref = refs_in[:3]
        self.o_ref = o_ref
        self.s_scr, self.m_scr = scratch[0:2], scratch[2:4]
        self.out_off = ctx if with_ctx else 0
        assert cfg.heads % 2 == 0
        if diff_scale is not None:
            lam_ref, self.ng_ref = refs_in[3:5]
            lp = lam_ref[...]
            self.lam = (jnp.exp(jnp.sum(lp[0:1] * lp[1:2], axis=-1, keepdims=True))
                        - jnp.exp(jnp.sum(lp[2:3] * lp[3:4], axis=-1, keepdims=True)) + diff_scale[0])

    @staticmethod
    def lane_range(lo, width, shape):
        lanes = _iota(shape, 1)
        return jnp.logical_and(lanes >= lo, lanes < lo + width)

    def scores(self, qb, kb, h):
        cfg, out = self.cfg, []
        for j in range(cfg.maps):
            sel = self.lane_range(h * cfg.head_stride + j * cfg.map_width, cfg.map_width, qb.shape)
            qm = jnp.where(sel, qb, jnp.zeros_like(qb))
            s = lax.dot_general(qm, kb, (((1,), (1,)), ((), ())), preferred_element_type=F32)
            out.append((s, jnp.max(s, axis=-1, keepdims=True)))
        return out

    def head_output(self, sm, v, h):
        probs = []
        for s, m in sm:
            e = jnp.exp2(s - m)
            probs.append((e.astype(BF16), 1.0 / jnp.sum(e, axis=-1, keepdims=True)))
        if self.cfg.maps == 1:
            e, inv = probs[0]
            o = _dot(e, v) * inv
        else:
            (e1, inv1), (e2, inv2) = probs
            o = _dot(e1 * inv1.astype(BF16) - e2 * (self.lam * inv2).astype(BF16), v)
        return jnp.where(self.lane_range(h * HEAD_DIM, HEAD_DIM, o.shape), o, 0.0)

    def issue_scores(self, qi, h, slot):
        blk, h_in = divmod(h, self.cfg.heads_per_block)
        row0 = pl.multiple_of(self.ctx + qi * self.tq, self.tq)
        sm = self.scores(self.q_ref[0, blk, pl.ds(row0, self.tq), :], self.k_ref[0, blk], h_in)
        for j, (s, m) in enumerate(sm):
            self.s_scr[slot][j] = s
            self.m_scr[slot][j] = m

    def slot_output(self, h):
        slot = h % 2
        sm = [(self.s_scr[slot][j], self.m_scr[slot][j]) for j in range(self.cfg.maps)]
        return self.head_output(sm, self.v_ref[0], h)

    def store(self, rows, acc):
        if self.diff_scale is not None:
            acc = _rms_segments(acc, self.ng_ref[...], HEAD_DIM) * self.diff_scale[1]
        self.o_ref[0, rows, :] = acc.astype(self.o_ref.dtype)

    def context_block(self):
        ctx = self.ctx
        acc = jnp.zeros((ctx, GROUP), F32)
        for h in range(self.cfg.heads):
            blk, h_in = divmod(h, self.cfg.heads_per_block)
            sm = self.scores(self.q_ref[0, blk, 0:ctx, :], self.k_ref[0, blk, 0:ctx, :], h_in)
            acc = acc + self.head_output(sm, self.v_ref[0, 0:ctx, :], h)
        self.store(slice(0, ctx), acc)


def _attn_kernel(*refs, groups, ctx, with_ctx, tq):
    n_in = [3 if ds is None else 5 for _, ds in groups]
    ins = [refs[sum(n_in[:i]):sum(n_in[:i + 1])] for i in range(len(groups))]
    outs = refs[sum(n_in):sum(n_in) + len(groups)]
    scr = refs[sum(n_in) + len(groups):]
    gs = [_AttnGroup(cfg, ds, ins[i], outs[i], scr[4 * i:4 * i + 4], ctx, with_ctx, tq)
          for i, (cfg, ds) in enumerate(groups)]
    n_heads = groups[0][0].heads
    assert all(cfg.heads == n_heads for cfg, _ in groups)
    n_blocks = (gs[0].k_ref.shape[2] - ctx) // tq

    def block_body(qi, carry, last=False):
        accs = [jnp.zeros((tq, GROUP), F32) for _ in gs]
        for h in range(n_heads):
            for g in gs:
                if h + 1 < n_heads:
                    g.issue_scores(qi, h + 1, (h + 1) % 2)
                elif not last:
                    g.issue_scores(qi + 1, 0, 0)
            accs = [acc + g.slot_output(h) for g, acc in zip(gs, accs)]
        for g, acc in zip(gs, accs):
            g.store(pl.ds(pl.multiple_of(g.out_off + qi * tq, tq), tq), acc)
        return carry

    for g in gs:
        g.issue_scores(0, 0, 0)
    lax.fori_loop(0, n_blocks - 1, block_body, 0)
    block_body(n_blocks - 1, 0, last=True)
    if with_ctx:
        for g in gs:
            g.context_block()


def _attention(inputs, *, groups, ctx, with_ctx, name):
    b, _, t, _ = inputs[0][0].shape
    tq = ATTN_Q_TILE
    assert (t - ctx) % tq == 0
    out_rows = t if with_ctx else t - ctx
    in_specs, args, scratch = [], [], []
    for (q, k, v, *extra), (cfg, _) in zip(inputs, groups):
        qk_spec = pl.BlockSpec((1, q.shape[1], t, GROUP), lambda bb: (bb, 0, 0, 0))
        in_specs += [qk_spec, qk_spec, pl.BlockSpec((1, t, GROUP), lambda bb: (bb, 0, 0))]
        in_specs += [pl.BlockSpec(e.shape, lambda bb: (0, 0)) for e in extra]
        args += [q, k, v, *extra]
        scratch += [pltpu.VMEM((cfg.maps, tq, t), F32)] * 2 + [pltpu.VMEM((cfg.maps, tq, 1), F32)] * 2
    return pl.pallas_call(
        functools.partial(_attn_kernel, groups=groups, ctx=ctx, with_ctx=with_ctx, tq=tq),
        grid=(b,),
        in_specs=in_specs,
        out_specs=[pl.BlockSpec((1, out_rows, GROUP), lambda bb: (bb, 0, 0))] * len(groups),
        out_shape=[jax.ShapeDtypeStruct((b, out_rows, GROUP), BF16)] * len(groups),
        scratch_shapes=scratch,
        compiler_params=pltpu.CompilerParams(dimension_semantics=("parallel",),
                                             vmem_limit_bytes=VMEM_LIMIT),
        name=name,
    )(*args)


def _out_kernel(h_ref, ya_ref, yb_ref, yc_ref, yd_ref, modb_ref, modc_ref, n2_ref,
                wo_ref, w1_ref, w2_ref, fg_ref, o_ref, *, final, first_tile, n_ctx_tiles):
    d_model = h_ref.shape[-1]
    n_sub = h_ref.shape[1] // ROW_TILE
    for k in range(n_sub):
        rows = slice(k * ROW_TILE, (k + 1) * ROW_TILE)
        is_ctx = jnp.where(first_tile + pl.program_id(0) * n_sub + k < n_ctx_tiles, 1.0, 0.0)
        mod = is_ctx * modc_ref[0] + (1.0 - is_ctx) * modb_ref[0]
        gate1, shift2, scale2, gate2 = (mod[:, i * d_model:(i + 1) * d_model] for i in range(2, 6))
        ycat = jnp.concatenate([ya_ref[0, rows, :], yb_ref[0, rows, :], yc_ref[0, rows, :], yd_ref[0, rows, :]],
                               axis=-1)
        h1 = h_ref[0, rows, :] + gate1 * _dot(ycat, wo_ref[...])
        t = _rms_rows(h1, n2_ref[...], d_model) * (1.0 + scale2) + shift2
        mid = jnp.maximum(_dot(t.astype(BF16), w1_ref[...]), 0.0)
        h2 = h1 + gate2 * _dot((mid * mid).astype(BF16), w2_ref[...])
        if final:
            h2 = _rms_rows(h2, fg_ref[...], d_model)
        o_ref[0, rows, :] = h2


def _out_block(hh, ys, mod3, n2, w_out, w1, w2, fg, *, layer, ctx, batch, final):
    b, t, d = hh.shape
    d_ff = w1.shape[-1]
    out_rows = t - ctx if final else t
    step_rows = next(r for r in (3 * ROW_TILE, 2 * ROW_TILE, ROW_TILE)
                     if out_rows % r == 0 and (not final or ctx % r == 0))
    off = ctx // step_rows if final else 0
    nt = out_rows // step_rows

    row_spec = lambda w, o: pl.BlockSpec((1, step_rows, w), lambda i, bb: (bb, i + o, 0))
    mod_spec = lambda r: pl.BlockSpec((1, 1, 6 * d), lambda i, bb: (bb if r is None else r, 0, 0))
    const2 = lambda r, c: pl.BlockSpec((r, c), lambda i, bb: (0, 0), pipeline_mode=pl.Buffered(1))
    layer_w = lambda r, c: pl.BlockSpec((None, r, c), lambda i, bb: (layer, 0, 0), pipeline_mode=pl.Buffered(1))
    return pl.pallas_call(
        functools.partial(_out_kernel, final=final, first_tile=off * (step_rows // ROW_TILE),
                          n_ctx_tiles=ctx // ROW_TILE),
        grid=(nt, b),
        in_specs=[row_spec(d, off), row_spec(GROUP, off)] + [row_spec(GROUP, 0)] * 3
                 + [mod_spec(None), mod_spec(batch)]
                 + [const2(1, d), layer_w(d, d), layer_w(d, d_ff), layer_w(d_ff, d), const2(1, d)],
        out_specs=pl.BlockSpec((1, step_rows, d), lambda i, bb: (bb, i, 0)),
        out_shape=jax.ShapeDtypeStruct((b, out_rows, d), F32),
        compiler_params=pltpu.CompilerParams(dimension_semantics=("parallel", "parallel"),
                                             vmem_limit_bytes=VMEM_LIMIT),
        name="out_block",
    )(hh, *ys, mod3, mod3, n2, w_out, w1, w2, fg)


def kernel(x, c, ctx, c_ctx, mod_w, mod_b, norm1_g, norm2_g, w_in, ssd_conv_w, ssd_conv_b, ssd_dt_bias,
           ssd_a_log, ssd_d, ssd_norm_g, diff_lambda, diff_norm_g, gqa_q_norm, gqa_k_norm, mla_q_norm,
           mla_kv_norm, mla_w_uq, mla_w_ukv, w_out, mlp_w1, mlp_w2, final_norm_g):
    batch, seq, d_model = x.shape
    n_ctx = ctx.shape[1]
    depth = mod_w.shape[0]
    assert n_ctx % ROW_TILE == 0 and seq % ROW_TILE == 0 and seq % GRID_W == 0
    assert w_in.shape[-1] == 2408 and d_model == 4 * GROUP

    pad_rows = -(batch + 1) % 8
    cond_rows = jnp.concatenate([c, c_ctx[None, :], jnp.zeros((pad_rows, d_model), F32)], axis=0)
    mod_all = _modulation(cond_rows, mod_w, mod_b)

    w_in_p = _take_cols(w_in.astype(BF16), _in_col_sources())
    w_uq_p = jnp.pad(_take_cols(mla_w_uq.astype(BF16), _mla_uq_sources()), ((0, 0), (0, 256 - MLA_Q_LORA), (0, 0)))
    w_ukv_p = _take_cols(mla_w_ukv.astype(BF16), _mla_ukv_sources())
    w_out_b = w_out.astype(BF16)
    w1_b = mlp_w1.astype(BF16)
    w2_b = mlp_w2.astype(BF16)

    slots32 = [s * 32 for s in range(8)]
    slots64 = [s * 64 for s in range(4)]
    slots_mla = [blk * 256 + hh * MLA_HEAD_PAD + MLA_NOPE for blk in range(2) for hh in range(2)]
    tables = (*_rope_tables(seq, n_ctx, DIFF_QK, slots32, 256),
              *_rope_tables(seq, n_ctx, HEAD_DIM, slots64, 256),
              *_rope_tables(seq, n_ctx, MLA_ROPE, slots_mla, 512))

    tile4 = lambda g: jnp.tile(g, 4)[None, :]
    hh = jnp.concatenate([ctx, x], axis=1)
    for i in range(depth):
        final = i == depth - 1
        with_ctx = not final
        lam_init = 0.8 - 0.6 * math.exp(-0.3 * i)
        mod3 = mod_all[i].reshape(mod_all.shape[1], 1, 6 * d_model)
        mq = jnp.pad(mla_q_norm[i], (0, 256 - MLA_Q_LORA))[None, :]
        (u_ssd, qd, kd, vd, qg, kg, vg, qm, km, vm) = _in_projection(
            hh, mod3, norm1_g[i][None, :], w_in_p, w_uq_p, w_ukv_p,
            tile4(gqa_q_norm[i]), tile4(gqa_k_norm[i]), mq, mla_kv_norm[i][None, :], tables,
            layer=i, ctx=n_ctx, batch=batch)

        pad8 = lambda v: jnp.pad(v.reshape(-1), (0, LANES - 2 * SSD_HEADS))[None, :]
        dt_bias2 = jnp.pad(jnp.tile(ssd_dt_bias[i].reshape(-1), 2), (0, LANES - 4 * SSD_HEADS))[None, :]
        y_a = _ssd(u_ssd, ssd_conv_w[i].T, ssd_conv_b[i][None, :], dt_bias2, pad8(ssd_a_log[i]),
                   jnp.repeat(ssd_d[i], HEAD_DIM, axis=-1), ssd_norm_g[i][None, :], ctx=n_ctx)
        (y_b,) = _attention([(qd, kd, vd, diff_lambda[i], tile4(diff_norm_g[i]))],
                            groups=((DIFF_CFG, (lam_init, 1.0 - lam_init)),),
                            ctx=n_ctx, with_ctx=with_ctx, name="diff_attention")
        y_c, y_d = _attention([(qg, kg, vg), (qm, km, vm)], groups=((GQA_CFG, None), (MLA_CFG, None)),
                              ctx=n_ctx, with_ctx=with_ctx, name="softmax_attention")
        hh = _out_block(hh, (y_a, y_b, y_c, y_d), mod3, norm2_g[i][None, :], w_out_b, w1_b, w2_b,
                        final_norm_g[None, :], layer=i, ctx=n_ctx, batch=batch, final=final)
    return hh
```

```python
import functools
import math
from typing import NamedTuple

import jax
import jax.numpy as jnp
import numpy as np
from jax import lax
from jax.experimental import pallas as pl
from jax.experimental.pallas import tpu as pltpu

F32 = jnp.float32
BF16 = jnp.bfloat16

LANES = 128
ROW_TILE = 256
ATTN_Q_TILE = 256
VMEM_LIMIT = 56 * 1024 * 1024

GRID_W = 64
ROPE_BASE = 10000.0
EPS = 1e-6
LOG2E = math.log2(math.e)
NEG_BIG = -1e30

HEAD_DIM = 64
GROUP = 256
V_ONES_W = GROUP + 128
SSD_CHUNK = 128
SSD_HEADS = 4
SSD_STATE = 64
SSD_CONV_CH = 512
DIFF_QK = 32
MLA_Q_LORA = 192
MLA_KV_LORA = 128
MLA_ROPE = 32
MLA_NOPE = 64
MLA_HEAD_PAD = 96

U_SSD = 0
U_SSD_W = 896
U_DIFF = U_SSD + U_SSD_W
U_GQA = U_DIFF + 768
U_MLA = U_GQA + 768
U_COLS = U_MLA + 640


def _in_col_sources():
    src = np.full((U_COLS,), -1, np.int64)
    src[0:776] = np.arange(776)
    src[776:784] = np.arange(768, 776)
    base = 776
    src[U_DIFF:U_DIFF + 768] = base + np.arange(768)
    base = 776 + 768
    src[U_GQA:U_GQA + 256] = base + np.arange(256)
    dup = np.concatenate([np.arange(64), np.arange(64), 64 + np.arange(64), 64 + np.arange(64)])
    src[U_GQA + 256:U_GQA + 512] = base + 256 + dup
    src[U_GQA + 512:U_GQA + 768] = base + 384 + dup
    base = 776 + 768 + 512
    src[U_MLA:U_MLA + MLA_Q_LORA] = base + np.arange(MLA_Q_LORA)
    src[U_MLA + 256:U_MLA + 384] = base + MLA_Q_LORA + np.arange(MLA_KV_LORA)
    kr = base + MLA_Q_LORA + MLA_KV_LORA + np.arange(MLA_ROPE)
    for h in range(2):
        off = U_MLA + 384 + h * MLA_HEAD_PAD + MLA_NOPE
        src[off:off + MLA_ROPE] = kr
    return src


def _take_cols(w, src):
    parts = []
    start = 0
    for i in range(1, len(src) + 1):
        run_ends = (i == len(src) or (src[i] < 0) != (src[start] < 0)
                    or (src[start] >= 0 and src[i] != src[i - 1] + 1))
        if run_ends:
            if src[start] < 0:
                parts.append(jnp.zeros(w.shape[:-1] + (i - start,), w.dtype))
            else:
                parts.append(w[..., int(src[start]):int(src[start]) + i - start])
            start = i
    return jnp.concatenate(parts, axis=-1)


def _mla_uq_sources():
    src = np.full((512,), -1, np.int64)
    src[0:192] = np.arange(192)
    src[256:448] = 192 + np.arange(192)
    return src


def _mla_ukv_sources():
    src = np.full((768,), -1, np.int64)
    for h in range(4):
        off = (h // 2) * 256 + (h % 2) * MLA_HEAD_PAD
        src[off:off + MLA_NOPE] = h * 128 + np.arange(64)
        src[512 + h * 64:512 + (h + 1) * 64] = h * 128 + 64 + np.arange(64)
    return src


def _rope_tables(seq, ctx, rot_dim, lane_slots, width):
    n_freq = rot_dim // 4
    half = rot_dim // 2
    freq = np.zeros((width,), np.int64)
    use_row = np.zeros((width,), np.float32)
    use_col = np.zeros((width,), np.float32)
    sign = np.zeros((width,), np.float32)
    for off in lane_slots:
        for o in range(rot_dim):
            i = o % half
            freq[off + o] = i % n_freq
            use_row[off + o] = 1.0 if i < n_freq else 0.0
            use_col[off + o] = 0.0 if i < n_freq else 1.0
            sign[off + o] = -1.0 if o < half else 1.0
    pos_row = np.concatenate([np.zeros(ctx), np.repeat(np.arange(seq // GRID_W), GRID_W)]).astype(np.float32)
    pos_col = np.concatenate([np.zeros(ctx), np.tile(np.arange(GRID_W), seq // GRID_W)]).astype(np.float32)
    inv = ROPE_BASE ** (-jnp.arange(n_freq, dtype=F32) / n_freq)
    inv_lane = jnp.take(inv, jnp.asarray(freq))
    ang = (jnp.asarray(pos_row)[:, None] * (inv_lane * use_row)[None, :]
           + jnp.asarray(pos_col)[:, None] * (inv_lane * use_col)[None, :])
    return jnp.cos(ang), jnp.sin(ang) * sign[None, :]


def _split3(v):
    hi = v.astype(BF16)
    r1 = v - hi.astype(F32)
    mid = r1.astype(BF16)
    lo = (r1 - mid.astype(F32)).astype(BF16)
    return hi, mid, lo


def _dot(a, b):
    return jnp.dot(a, b, preferred_element_type=F32)


def _dot_sel_rhs2(v, sel):
    hi = v.astype(BF16)
    lo = (v - hi.astype(F32)).astype(BF16)
    return _dot(hi, sel) + _dot(lo, sel)


def _dot_sel_lhs(sel, v):
    hi, mid, lo = _split3(v)
    return _dot(sel, hi) + _dot(sel, mid) + _dot(sel, lo)


def _iota(shape, axis):
    return lax.broadcasted_iota(jnp.int32, shape, axis)


def _seg_mean_matrix(width, seg):
    r = _iota((width, width), 0) // seg
    c = _iota((width, width), 1) // seg
    return jnp.where(r == c, 1.0 / seg, 0.0).astype(BF16)


def _rms_rows(x, g, n):
    ms = jnp.sum(x * x, axis=-1, keepdims=True) * (1.0 / n)
    return x * lax.rsqrt(ms + EPS) * g


def _rms_segments(x, g, seg):
    ms = _dot_sel_rhs2(x * x, _seg_mean_matrix(x.shape[-1], seg))
    return x * lax.rsqrt(ms + EPS) * g


def _rope(x, cos, sin, rot_dim):
    half = rot_dim // 2
    outs = []
    for s in range(x.shape[-1] // LANES):
        xs = x[:, s * LANES:(s + 1) * LANES]
        up = pltpu.roll(xs, LANES - half, axis=1)
        dn = pltpu.roll(xs, half, axis=1)
        first = (_iota(xs.shape, 1) % rot_dim) < half
        rot = jnp.where(first, up, dn)
        outs.append(xs * cos[:, s * LANES:(s + 1) * LANES] + rot * sin[:, s * LANES:(s + 1) * LANES])
    return jnp.concatenate(outs, axis=-1) if len(outs) > 1 else outs[0]


def _with_ones_block(v):
    return jnp.concatenate([v[:, 0:LANES], jnp.ones((v.shape[0], LANES), v.dtype), v[:, LANES:2 * LANES]], axis=-1)


def _loop_unrolled(n, body, unroll=2):
    if n % unroll:
        unroll = 1

    def group(i, carry):
        for k in range(unroll):
            body(i * unroll + k, carry)
        return carry

    lax.fori_loop(0, n // unroll, group, 0)


def _silu(x):
    return x / (1.0 + jnp.exp(-x))


def _softplus(x):
    return jnp.maximum(x, 0.0) + jnp.log(1.0 + jnp.exp(-jnp.abs(x)))


def _mod_kernel(c_ref, w_ref, b_ref, o_ref):
    cond = _silu(c_ref[...]).astype(BF16)
    o_ref[0] = _dot(cond, w_ref[0].astype(BF16)) + b_ref[0]


def _modulation(cond_rows, mod_w, mod_b):
    depth, d, d6 = mod_w.shape
    rows = cond_rows.shape[0]
    return pl.pallas_call(
        _mod_kernel,
        grid=(depth, d6 // d),
        in_specs=[pl.BlockSpec((rows, d), lambda i, j: (0, 0)),
                  pl.BlockSpec((1, d, d), lambda i, j: (i, 0, j)),
                  pl.BlockSpec((1, 1, d), lambda i, j: (i, 0, j))],
        out_specs=pl.BlockSpec((1, rows, d), lambda i, j: (i, 0, j)),
        out_shape=jax.ShapeDtypeStruct((depth, rows, d6), F32),
        compiler_params=pltpu.CompilerParams(dimension_semantics=("parallel", "parallel"),
                                             vmem_limit_bytes=VMEM_LIMIT),
        name="modulation",
    )(cond_rows, mod_w, mod_b.reshape(depth, 1, d6))


def _in_kernel(h_ref, shb_ref, scb_ref, shc_ref, scc_ref, g_ref, w_ref, wuq_ref, wukv_ref,
               gq_ref, gk_ref, mq_ref, mkv_ref, cd_ref, sd_ref, cg_ref, sg_ref, cm_ref, sm_ref,
               ussd_ref, qd_ref, kd_ref, vd_ref, qg_ref, kg_ref, vg_ref, qm_ref, km_ref, vm_ref, *, n_ctx_tiles):
    d_model = h_ref.shape[-1]
    for k in range(h_ref.shape[1] // ROW_TILE):
        rows = slice(k * ROW_TILE, (k + 1) * ROW_TILE)
        is_ctx = jnp.where(pl.program_id(0) * (h_ref.shape[1] // ROW_TILE) + k < n_ctx_tiles, 1.0, 0.0)
        shift = is_ctx * shc_ref[0] + (1.0 - is_ctx) * shb_ref[0]
        scale = is_ctx * scc_ref[0] + (1.0 - is_ctx) * scb_ref[0]
        xn = _rms_rows(h_ref[0, rows, :], g_ref[...], d_model) * (1.0 + scale) + shift
        u = _dot(xn.astype(BF16), w_ref[...])
        ussd_ref[0, rows, :] = u[:, U_SSD:U_SSD + U_SSD_W]

        q = _rope(u[:, U_DIFF:U_DIFF + 256], cd_ref[rows, :], sd_ref[rows, :], DIFF_QK)
        kk = _rope(u[:, U_DIFF + 256:U_DIFF + 512], cd_ref[rows, :], sd_ref[rows, :], DIFF_QK)
        qd_ref[0, 0, rows, :] = (q * (DIFF_QK ** -0.5 * LOG2E)).astype(BF16)
        kd_ref[0, 0, rows, :] = kk.astype(BF16)
        vd_ref[0, rows, :] = _with_ones_block(u[:, U_DIFF + 512:U_DIFF + 768].astype(BF16))

        q = _rms_segments(u[:, U_GQA:U_GQA + 256], gq_ref[...], HEAD_DIM)
        kk = _rms_segments(u[:, U_GQA + 256:U_GQA + 512], gk_ref[...], HEAD_DIM)
        q = _rope(q, cg_ref[rows, :], sg_ref[rows, :], HEAD_DIM)
        kk = _rope(kk, cg_ref[rows, :], sg_ref[rows, :], HEAD_DIM)
        qg_ref[0, 0, rows, :] = (q * (HEAD_DIM ** -0.5 * LOG2E)).astype(BF16)
        kg_ref[0, 0, rows, :] = kk.astype(BF16)
        vg_ref[0, rows, :] = _with_ones_block(u[:, U_GQA + 512:U_GQA + 768].astype(BF16))

        cq = _rms_rows(u[:, U_MLA:U_MLA + 256], mq_ref[...], MLA_Q_LORA)
        q = _rope(_dot(cq.astype(BF16), wuq_ref[...]), cm_ref[rows, :], sm_ref[rows, :], MLA_ROPE)
        q = (q * ((MLA_NOPE + MLA_ROPE) ** -0.5 * LOG2E)).astype(BF16)
        qm_ref[0, 0, rows, :] = q[:, 0:GROUP]
        qm_ref[0, 1, rows, :] = q[:, GROUP:2 * GROUP]
        ckv = _rms_rows(u[:, U_MLA + 256:U_MLA + 384], mkv_ref[...], MLA_KV_LORA)
        kv = _dot(ckv.astype(BF16), wukv_ref[...])
        kr = _rope(u[:, U_MLA + 384:U_MLA + 640], cm_ref[rows, 0:GROUP], sm_ref[rows, 0:GROUP], MLA_ROPE)
        km_ref[0, 0, rows, :] = (kv[:, 0:GROUP] + kr).astype(BF16)
        km_ref[0, 1, rows, :] = (kv[:, GROUP:2 * GROUP] + kr).astype(BF16)
        vm_ref[0, rows, :] = _with_ones_block(kv[:, 512:768].astype(BF16))


def _in_projection(hh, mod3, g1, w_in, w_uq, w_ukv, gq, gk, mq, mkv, tables, *, layer, ctx, batch):
    b, t, d = hh.shape
    step_rows = next(r for r in (3 * ROW_TILE, 2 * ROW_TILE, ROW_TILE) if t % r == 0)
    nt = t // step_rows

    row_spec = lambda w: pl.BlockSpec((1, step_rows, w), lambda i, bb: (bb, i, 0))
    tab_spec = lambda w: pl.BlockSpec((step_rows, w), lambda i, bb: (i, 0))
    const2 = lambda r, c: pl.BlockSpec((r, c), lambda i, bb: (0, 0))
    layer_w = lambda r, c: pl.BlockSpec((None, r, c), lambda i, bb: (layer, 0, 0))
    in_specs = [
        row_spec(d),
        pl.BlockSpec((1, 1, d), lambda i, bb: (bb, 0, 0)),
        pl.BlockSpec((1, 1, d), lambda i, bb: (bb, 0, 1)),
        pl.BlockSpec((1, 1, d), lambda i, bb: (batch, 0, 0)),
        pl.BlockSpec((1, 1, d), lambda i, bb: (batch, 0, 1)),
        const2(1, d), layer_w(d, U_COLS), layer_w(256, 512), layer_w(128, 768),
        const2(1, 256), const2(1, 256), const2(1, 256), const2(1, 128),
        tab_spec(256), tab_spec(256), tab_spec(256), tab_spec(256), tab_spec(512), tab_spec(512),
    ]
    qk_spec = lambda n: pl.BlockSpec((1, n, step_rows, GROUP), lambda i, bb: (bb, 0, i, 0))
    qk_shape = lambda n: jax.ShapeDtypeStruct((b, n, t, GROUP), BF16)
    v_shape = jax.ShapeDtypeStruct((b, t, V_ONES_W), BF16)
    return pl.pallas_call(
        functools.partial(_in_kernel, n_ctx_tiles=ctx // ROW_TILE),
        grid=(nt, b),
        in_specs=in_specs,
        out_specs=[row_spec(U_SSD_W), qk_spec(1), qk_spec(1), row_spec(V_ONES_W), qk_spec(1), qk_spec(1),
                   row_spec(V_ONES_W), qk_spec(2), qk_spec(2), row_spec(V_ONES_W)],
        out_shape=[jax.ShapeDtypeStruct((b, t, U_SSD_W), F32), qk_shape(1), qk_shape(1), v_shape,
                   qk_shape(1), qk_shape(1), v_shape, qk_shape(2), qk_shape(2), v_shape],
        compiler_params=pltpu.CompilerParams(dimension_semantics=("parallel", "parallel"),
                                             vmem_limit_bytes=VMEM_LIMIT),
        name="in_projection",
    )(hh, mod3, mod3, mod3, mod3, g1, w_in, w_uq, w_ukv, gq, gk, mq, mkv, *tables)


def _ssd_kernel(u_ref, cw_ref, cb_ref, dtb_ref, alog_ref, dsk_ref, ng_ref, y_ref,
                xs_s, c_s, yf_s, yb_s, upd_s, win_s, st_s, *, t_rows, ctx):
    L = SSD_CHUNK
    nc = t_rows // L
    ncc = ctx // L
    n_cols = 2 * SSD_HEADS
    row = _iota((L, SSD_CONV_CH), 0)
    lane = _iota((1, LANES), 1)
    a_neg = jnp.where(lane < n_cols, -jnp.exp(alog_ref[...]), 0.0)
    r128 = _iota((L, L), 0)
    c128 = _iota((L, L), 1)
    lane_l = _iota((L, LANES), 1)
    lane256 = _iota((L, GROUP), 1)
    causal = (r128 >= c128, r128 <= c128)
    tri = jnp.where(causal[0], 1.0, 0.0).astype(BF16)
    bd_mask = (_iota((L, GROUP), 0) // SSD_STATE) == (lane256 // (2 * HEAD_DIM))
    group_lanes = [c128 // SSD_STATE == g for g in range(2)]
    head_lanes = [lane256 // HEAD_DIM == h for h in range(SSD_HEADS)]
    sel = [jnp.where(_iota((LANES, GROUP), 0) == d * SSD_HEADS + _iota((LANES, GROUP), 1) // HEAD_DIM,
                     1.0, 0.0).astype(BF16) for d in range(2)]

    def chunk_body(c, carry):
        start = pl.multiple_of(c * L, L)
        x = u_ref[0, pl.ds(start, L), 256:768]
        first = jnp.logical_or(c == 0, c == ncc)
        last = jnp.logical_or(c == ncc - 1, c == nc - 1)
        prev8 = u_ref[0, pl.ds(pl.multiple_of(jnp.maximum(start - 8, 0), 8), 8), 256:768]
        next8 = u_ref[0, pl.ds(pl.multiple_of(jnp.minimum(start + L, t_rows - 8), 8), 8), 256:768]
        prev_row = prev8[7:8, :] * jnp.where(first, 0.0, 1.0)
        next_row = next8[0:1, :] * jnp.where(last, 0.0, 1.0)
        xp = jnp.where(row == 0, prev_row, pltpu.roll(x, 1, axis=0))
        xn = jnp.where(row == L - 1, next_row, pltpu.roll(x, L - 1, axis=0))
        xbc = _silu(cw_ref[0:1, :] * xp + cw_ref[1:2, :] * x + cw_ref[2:3, :] * xn + cb_ref[...])
        xs, bm, cm = xbc[:, 0:256], xbc[:, 256:384], xbc[:, 384:512]
        dtp = _softplus(u_ref[0, pl.ds(start, L), 768:896] + dtb_ref[...])

        a = dtp * a_neg
        cs_f = _dot_sel_lhs(tri, a)
        tot = cs_f[L - 1:L, :]
        cs = jnp.where(lane_l < SSD_HEADS, cs_f, tot - cs_f + a)
        in_dec = jnp.exp(cs)
        w_end = dtp * jnp.exp(tot - cs)
        rows_t = jnp.where(lane_l < n_cols, cs, dtp).T

        xs16 = xs.astype(BF16)
        cb16 = cm.astype(BF16)
        bt16 = bm.T.astype(BF16)
        gram = [_dot(jnp.where(group_lanes[g], cb16, jnp.zeros_like(cb16)), bt16) for g in range(2)]
        x_heads = jnp.concatenate([jnp.where(head_lanes[h], xs16, jnp.zeros_like(xs16))
                                   for h in range(SSD_HEADS)], axis=0)
        for d, y_dst in ((0, yf_s), (1, yb_s)):
            mixes = []
            for h in range(SSD_HEADS):
                j = d * SSD_HEADS + h
                delta = cs[:, j:j + 1] - rows_t[j:j + 1, :]
                decay = jnp.exp(jnp.where(causal[d], delta, NEG_BIG))
                mixes.append((gram[h // 2] * decay * rows_t[n_cols + j:n_cols + j + 1, :]).astype(BF16))
            y_dst[pl.ds(start, L), :] = _dot(jnp.concatenate(mixes, axis=1), x_heads)
            win_s[d, pl.ds(start, L), :] = _dot_sel_rhs2(in_dec, sel[d])
            x_st = (xs * _dot_sel_rhs2(w_end, sel[d])).astype(BF16)
            upd_s[d, pl.ds(start, L), :] = jnp.where(bd_mask, _dot(bt16, x_st), 0.0)
        xs_s[pl.ds(start, L), :] = xs
        c_s[pl.ds(start, L), :] = cb16
        return carry

    _loop_unrolled(nc, chunk_body)

    st_s[...] = jnp.zeros_like(st_s)

    def carry_state(c, d, y_dst):
        start = pl.multiple_of(c * L, L)
        state = st_s[d]
        w_in = win_s[d, pl.ds(start, L), :]
        y_dst[pl.ds(start, L), :] += _dot(c_s[pl.ds(start, L), :], state.astype(BF16)) * w_in
        c_dec = w_in[L - 1:L, :] if d == 0 else w_in[0:1, :]
        st_s[d] = state * c_dec + upd_s[d, pl.ds(start, L), :]

    def scan_body(i, carry):
        carry_state(i, 0, yf_s)
        carry_state(jnp.where(i < ncc, ncc - 1 - i, nc - 1 - (i - ncc)), 1, yb_s)
        return carry

    _loop_unrolled(nc, scan_body)

    d_sum = dsk_ref[0:1, :] + dsk_ref[1:2, :]

    def finish_body(c, carry):
        start = pl.multiple_of(c * L, L)
        y = yf_s[pl.ds(start, L), :] + yb_s[pl.ds(start, L), :] + d_sum * xs_s[pl.ds(start, L), :]
        gated = y * _silu(u_ref[0, pl.ds(start, L), 0:256])
        y_ref[0, pl.ds(start, L), :] = _rms_rows(gated, ng_ref[...], GROUP).astype(y_ref.dtype)
        return carry

    _loop_unrolled(nc, finish_body)


def _ssd(u_ssd, conv_w, conv_b, dt_bias, a_log, d_skip, norm_g, *, ctx):
    b, t, w = u_ssd.shape
    const2 = lambda r, c: pl.BlockSpec((r, c), lambda bb: (0, 0))
    return pl.pallas_call(
        functools.partial(_ssd_kernel, t_rows=t, ctx=ctx),
        grid=(b,),
        in_specs=[pl.BlockSpec((1, t, w), lambda bb: (bb, 0, 0)),
                  const2(3, SSD_CONV_CH), const2(1, SSD_CONV_CH), const2(1, LANES), const2(1, LANES),
                  const2(2, GROUP), const2(1, GROUP)],
        out_specs=pl.BlockSpec((1, t, GROUP), lambda bb: (bb, 0, 0)),
        out_shape=jax.ShapeDtypeStruct((b, t, GROUP), BF16),
        scratch_shapes=[pltpu.VMEM((t, GROUP), F32), pltpu.VMEM((t, LANES), BF16),
                        pltpu.VMEM((t, GROUP), F32), pltpu.VMEM((t, GROUP), F32),
                        pltpu.VMEM((2, t, GROUP), F32), pltpu.VMEM((2, t, GROUP), F32),
                        pltpu.VMEM((2, SSD_CHUNK, GROUP), F32)],
        compiler_params=pltpu.CompilerParams(dimension_semantics=("parallel",),
                                             vmem_limit_bytes=VMEM_LIMIT),
        name="ssd_mixer",
    )(u_ssd, conv_w, conv_b, dt_bias, a_log, d_skip, norm_g)


class AttnCfg(NamedTuple):
    heads: int
    maps: int
    heads_per_block: int
    head_stride: int
    map_width: int


DIFF_CFG = AttnCfg(heads=4, maps=2, heads_per_block=4, head_stride=64, map_width=DIFF_QK)
GQA_CFG = AttnCfg(heads=4, maps=1, heads_per_block=4, head_stride=64, map_width=HEAD_DIM)
MLA_CFG = AttnCfg(heads=4, maps=1, heads_per_block=2, head_stride=MLA_HEAD_PAD, map_width=MLA_HEAD_PAD)


class _AttnGroup:
    def __init__(self, cfg, diff_scale, refs_in, o_ref, scratch, ctx, with_ctx, tq):
        self.cfg, self.diff_scale, self.ctx, self.tq = cfg, diff_scale, ctx, tq
        self.q_ref, self.k_ref, self.v_ref = refs_in[:3]
        self.o_ref = o_ref
        self.s_scr, self.m_scr = scratch[0:2], scratch[2:4]
        self.out_off = ctx if with_ctx else 0
        assert cfg.heads % 2 == 0
        if diff_scale is not None:
            lam_ref, self.ng_ref = refs_in[3:5]
            lp = lam_ref[...]
            self.lam = (jnp.exp(jnp.sum(lp[0:1] * lp[1:2], axis=-1, keepdims=True))
                        - jnp.exp(jnp.sum(lp[2:3] * lp[3:4], axis=-1, keepdims=True)) + diff_scale[0])

    @staticmethod
    def lane_range(lo, width, shape):
        lanes = _iota(shape, 1)
        return jnp.logical_and(lanes >= lo, lanes < lo + width)

    def scores(self, qb, kb, h):
        cfg, out = self.cfg, []
        for j in range(cfg.maps):
            sel = self.lane_range(h * cfg.head_stride + j * cfg.map_width, cfg.map_width, qb.shape)
            qm = jnp.where(sel, qb, jnp.zeros_like(qb))
            s = lax.dot_general(qm, kb, (((1,), (1,)), ((), ())), preferred_element_type=F32)
            out.append((s, jnp.max(s, axis=-1, keepdims=True)))
        return out

    def head_output(self, sm, v, h):
        lo = 0 if h < self.cfg.heads // 2 else LANES
        ones_lane = LANES - lo
        outs = []
        for s, m in sm:
            o = _dot(jnp.exp2(s - m).astype(BF16), v[:, lo:lo + GROUP])
            outs.append(o * (1.0 / o[:, ones_lane:ones_lane + 1]))
        o = outs[0] if self.cfg.maps == 1 else outs[0] - self.lam * outs[1]
        return jnp.where(self.lane_range(h * HEAD_DIM, HEAD_DIM, o.shape), o, 0.0)

    def issue_scores(self, qi, h, slot):
        blk, h_in = divmod(h, self.cfg.heads_per_block)
        row0 = pl.multiple_of(self.ctx + qi * self.tq, self.tq)
        sm = self.scores(self.q_ref[0, blk, pl.ds(row0, self.tq), :], self.k_ref[0, blk], h_in)
        for j, (s, m) in enumerate(sm):
            self.s_scr[slot][j] = s
            self.m_scr[slot][j] = m

    def slot_output(self, h):
        slot = h % 2
        sm = [(self.s_scr[slot][j], self.m_scr[slot][j]) for j in range(self.cfg.maps)]
        return self.head_output(sm, self.v_ref[0], h)

    def store(self, rows, acc):
        if self.diff_scale is not None:
            acc = _rms_segments(acc, self.ng_ref[...], HEAD_DIM) * self.diff_scale[1]
        self.o_ref[0, rows, :] = acc.astype(self.o_ref.dtype)

    def context_block(self):
        ctx = self.ctx
        acc = jnp.zeros((ctx, GROUP), F32)
        for h in range(self.cfg.heads):
            blk, h_in = divmod(h, self.cfg.heads_per_block)
            sm = self.scores(self.q_ref[0, blk, 0:ctx, :], self.k_ref[0, blk, 0:ctx, :], h_in)
            acc = acc + self.head_output(sm, self.v_ref[0, 0:ctx, :], h)
        self.store(slice(0, ctx), acc)


def _attn_kernel(*refs, groups, ctx, with_ctx, tq):
    n_in = [3 if ds is None else 5 for _, ds in groups]
    ins = [refs[sum(n_in[:i]):sum(n_in[:i + 1])] for i in range(len(groups))]
    outs = refs[sum(n_in):sum(n_in) + len(groups)]
    scr = refs[sum(n_in) + len(groups):]
    gs = [_AttnGroup(cfg, ds, ins[i], outs[i], scr[4 * i:4 * i + 4], ctx, with_ctx, tq)
          for i, (cfg, ds) in enumerate(groups)]
    n_heads = groups[0][0].heads
    assert all(cfg.heads == n_heads for cfg, _ in groups)
    n_blocks = (gs[0].k_ref.shape[2] - ctx) // tq

    def block_body(qi, carry, last=False):
        accs = [jnp.zeros((tq, GROUP), F32) for _ in gs]
        for h in range(n_heads):
            for g in gs:
                if h + 1 < n_heads:
                    g.issue_scores(qi, h + 1, (h + 1) % 2)
                elif not last:
                    g.issue_scores(qi + 1, 0, 0)
            accs = [acc + g.slot_output(h) for g, acc in zip(gs, accs)]
        for g, acc in zip(gs, accs):
            g.store(pl.ds(pl.multiple_of(g.out_off + qi * tq, tq), tq), acc)
        return carry

    for g in gs:
        g.issue_scores(0, 0, 0)
    lax.fori_loop(0, n_blocks - 1, block_body, 0)
    block_body(n_blocks - 1, 0, last=True)
    if with_ctx:
        for g in gs:
            g.context_block()


def _attention(inputs, *, groups, ctx, with_ctx, name):
    b, _, t, _ = inputs[0][0].shape
    tq = ATTN_Q_TILE
    assert (t - ctx) % tq == 0
    out_rows = t if with_ctx else t - ctx
    in_specs, args, scratch = [], [], []
    for (q, k, v, *extra), (cfg, _) in zip(inputs, groups):
        qk_spec = pl.BlockSpec((1, q.shape[1], t, GROUP), lambda bb: (bb, 0, 0, 0))
        in_specs += [qk_spec, qk_spec, pl.BlockSpec((1, t, v.shape[-1]), lambda bb: (bb, 0, 0))]
        in_specs += [pl.BlockSpec(e.shape, lambda bb: (0, 0)) for e in extra]
        args += [q, k, v, *extra]
        scratch += [pltpu.VMEM((cfg.maps, tq, t), F32)] * 2 + [pltpu.VMEM((cfg.maps, tq, 1), F32)] * 2
    return pl.pallas_call(
        functools.partial(_attn_kernel, groups=groups, ctx=ctx, with_ctx=with_ctx, tq=tq),
        grid=(b,),
        in_specs=in_specs,
        out_specs=[pl.BlockSpec((1, out_rows, GROUP), lambda bb: (bb, 0, 0))] * len(groups),
        out_shape=[jax.ShapeDtypeStruct((b, out_rows, GROUP), BF16)] * len(groups),
        scratch_shapes=scratch,
        compiler_params=pltpu.CompilerParams(dimension_semantics=("parallel",),
                                             vmem_limit_bytes=VMEM_LIMIT),
        name=name,
    )(*args)


def _out_kernel(h_ref, ya_ref, yb_ref, yc_ref, yd_ref, modb_ref, modc_ref, n2_ref,
                wo_ref, w1_ref, w2_ref, fg_ref, o_ref, *, final, first_tile, n_ctx_tiles):
    d_model = h_ref.shape[-1]
    n_sub = h_ref.shape[1] // ROW_TILE
    for k in range(n_sub):
        rows = slice(k * ROW_TILE, (k + 1) * ROW_TILE)
        is_ctx = jnp.where(first_tile + pl.program_id(0) * n_sub + k < n_ctx_tiles, 1.0, 0.0)
        mod = is_ctx * modc_ref[0] + (1.0 - is_ctx) * modb_ref[0]
        gate1, shift2, scale2, gate2 = (mod[:, i * d_model:(i + 1) * d_model] for i in range(2, 6))
        ycat = jnp.concatenate([ya_ref[0, rows, :], yb_ref[0, rows, :], yc_ref[0, rows, :], yd_ref[0, rows, :]],
                               axis=-1)
        h1 = h_ref[0, rows, :] + gate1 * _dot(ycat, wo_ref[...])
        t = _rms_rows(h1, n2_ref[...], d_model) * (1.0 + scale2) + shift2
        mid = jnp.maximum(_dot(t.astype(BF16), w1_ref[...]), 0.0)
        h2 = h1 + gate2 * _dot((mid * mid).astype(BF16), w2_ref[...])
        if final:
            h2 = _rms_rows(h2, fg_ref[...], d_model)
        o_ref[0, rows, :] = h2


def _out_block(hh, ys, mod3, n2, w_out, w1, w2, fg, *, layer, ctx, batch, final):
    b, t, d = hh.shape
    d_ff = w1.shape[-1]
    out_rows = t - ctx if final else t
    step_rows = next(r for r in (3 * ROW_TILE, 2 * ROW_TILE, ROW_TILE)
                     if out_rows % r == 0 and (not final or ctx % r == 0))
    off = ctx // step_rows if final else 0
    nt = out_rows // step_rows

    row_spec = lambda w, o: pl.BlockSpec((1, step_rows, w), lambda i, bb: (bb, i + o, 0))
    mod_spec = lambda r: pl.BlockSpec((1, 1, 6 * d), lambda i, bb: (bb if r is None else r, 0, 0))
    const2 = lambda r, c: pl.BlockSpec((r, c), lambda i, bb: (0, 0), pipeline_mode=pl.Buffered(1))
    layer_w = lambda r, c: pl.BlockSpec((None, r, c), lambda i, bb: (layer, 0, 0), pipeline_mode=pl.Buffered(1))
    return pl.pallas_call(
        functools.partial(_out_kernel, final=final, first_tile=off * (step_rows // ROW_TILE),
                          n_ctx_tiles=ctx // ROW_TILE),
        grid=(nt, b),
        in_specs=[row_spec(d, off), row_spec(GROUP, off)] + [row_spec(GROUP, 0)] * 3
                 + [mod_spec(None), mod_spec(batch)]
                 + [const2(1, d), layer_w(d, d), layer_w(d, d_ff), layer_w(d_ff, d), const2(1, d)],
        out_specs=pl.BlockSpec((1, step_rows, d), lambda i, bb: (bb, i, 0)),
        out_shape=jax.ShapeDtypeStruct((b, out_rows, d), F32),
        compiler_params=pltpu.CompilerParams(dimension_semantics=("parallel", "parallel"),
                                             vmem_limit_bytes=VMEM_LIMIT),
        name="out_block",
    )(hh, *ys, mod3, mod3, n2, w_out, w1, w2, fg)


def kernel(x, c, ctx, c_ctx, mod_w, mod_b, norm1_g, norm2_g, w_in, ssd_conv_w, ssd_conv_b, ssd_dt_bias,
           ssd_a_log, ssd_d, ssd_norm_g, diff_lambda, diff_norm_g, gqa_q_norm, gqa_k_norm, mla_q_norm,
           mla_kv_norm, mla_w_uq, mla_w_ukv, w_out, mlp_w1, mlp_w2, final_norm_g):
    batch, seq, d_model = x.shape
    n_ctx = ctx.shape[1]
    depth = mod_w.shape[0]
    assert n_ctx % ROW_TILE == 0 and seq % ROW_TILE == 0 and seq % GRID_W == 0
    assert w_in.shape[-1] == 2408 and d_model == 4 * GROUP

    pad_rows = -(batch + 1) % 8
    cond_rows = jnp.concatenate([c, c_ctx[None, :], jnp.zeros((pad_rows, d_model), F32)], axis=0)
    mod_all = _modulation(cond_rows, mod_w, mod_b)

    w_in_p = _take_cols(w_in.astype(BF16), _in_col_sources())
    w_uq_p = jnp.pad(_take_cols(mla_w_uq.astype(BF16), _mla_uq_sources()), ((0, 0), (0, 256 - MLA_Q_LORA), (0, 0)))
    w_ukv_p = _take_cols(mla_w_ukv.astype(BF16), _mla_ukv_sources())
    w_out_b = w_out.astype(BF16)
    w1_b = mlp_w1.astype(BF16)
    w2_b = mlp_w2.astype(BF16)

    slots32 = [s * 32 for s in range(8)]
    slots64 = [s * 64 for s in range(4)]
    slots_mla = [blk * 256 + hh * MLA_HEAD_PAD + MLA_NOPE for blk in range(2) for hh in range(2)]
    tables = (*_rope_tables(seq, n_ctx, DIFF_QK, slots32, 256),
              *_rope_tables(seq, n_ctx, HEAD_DIM, slots64, 256),
              *_rope_tables(seq, n_ctx, MLA_ROPE, slots_mla, 512))

    tile4 = lambda g: jnp.tile(g, 4)[None, :]
    hh = jnp.concatenate([ctx, x], axis=1)
    for i in range(depth):
        final = i == depth - 1
        with_ctx = not final
        lam_init = 0.8 - 0.6 * math.exp(-0.3 * i)
        mod3 = mod_all[i].reshape(mod_all.shape[1], 1, 6 * d_model)
        mq = jnp.pad(mla_q_norm[i], (0, 256 - MLA_Q_LORA))[None, :]
        (u_ssd, qd, kd, vd, qg, kg, vg, qm, km, vm) = _in_projection(
            hh, mod3, norm1_g[i][None, :], w_in_p, w_uq_p, w_ukv_p,
            tile4(gqa_q_norm[i]), tile4(gqa_k_norm[i]), mq, mla_kv_norm[i][None, :], tables,
            layer=i, ctx=n_ctx, batch=batch)

        pad8 = lambda v: jnp.pad(v.reshape(-1), (0, LANES - 2 * SSD_HEADS))[None, :]
        dt_bias2 = jnp.pad(jnp.tile(ssd_dt_bias[i].reshape(-1), 2), (0, LANES - 4 * SSD_HEADS))[None, :]
        y_a = _ssd(u_ssd, ssd_conv_w[i].T, ssd_conv_b[i][None, :], dt_bias2, pad8(ssd_a_log[i]),
                   jnp.repeat(ssd_d[i], HEAD_DIM, axis=-1), ssd_norm_g[i][None, :], ctx=n_ctx)
        (y_b,) = _attention([(qd, kd, vd, diff_lambda[i], tile4(diff_norm_g[i]))],
                            groups=((DIFF_CFG, (lam_init, 1.0 - lam_init)),),
                            ctx=n_ctx, with_ctx=with_ctx, name="diff_attention")
        y_c, y_d = _attention([(qg, kg, vg), (qm, km, vm)], groups=((GQA_CFG, None), (MLA_CFG, None)),
                              ctx=n_ctx, with_ctx=with_ctx, name="softmax_attention")
        hh = _out_block(hh, (y_a, y_b, y_c, y_d), mod3, norm2_g[i][None, :], w_out_b, w1_b, w2_b,
                        final_norm_g[None, :], layer=i, ctx=n_ctx, batch=batch, final=final)
    return hh
```

```python
import functools
import math
from typing import NamedTuple

import jax
import jax.numpy as jnp
import numpy as np
from jax import lax
from jax.experimental import pallas as pl
from jax.experimental.pallas import tpu as pltpu

F32 = jnp.float32
BF16 = jnp.bfloat16

LANES = 128
ROW_TILE = 256
ATTN_Q_TILE = 256
VMEM_LIMIT = 56 * 1024 * 1024

GRID_W = 64
ROPE_BASE = 10000.0
EPS = 1e-6
LOG2E = math.log2(math.e)
NEG_BIG = -1e30

HEAD_DIM = 64
GROUP = 256
V_ONES_W = GROUP + 128
SSD_CHUNK = 128
SSD_HEADS = 4
SSD_STATE = 64
SSD_CONV_CH = 512
DIFF_QK = 32
MLA_Q_LORA = 192
MLA_KV_LORA = 128
MLA_ROPE = 32
MLA_NOPE = 64
MLA_HEAD_PAD = 96

U_SSD = 0
U_SSD_W = 896
U_DIFF = U_SSD + U_SSD_W
U_GQA = U_DIFF + 768
U_MLA = U_GQA + 768
U_COLS = U_MLA + 640


def _in_col_sources():
    src = np.full((U_COLS,), -1, np.int64)
    src[0:776] = np.arange(776)
    src[776:784] = np.arange(768, 776)
    base = 776
    src[U_DIFF:U_DIFF + 768] = base + np.arange(768)
    base = 776 + 768
    src[U_GQA:U_GQA + 256] = base + np.arange(256)
    dup = np.concatenate([np.arange(64), np.arange(64), 64 + np.arange(64), 64 + np.arange(64)])
    src[U_GQA + 256:U_GQA + 512] = base + 256 + dup
    src[U_GQA + 512:U_GQA + 768] = base + 384 + dup
    base = 776 + 768 + 512
    src[U_MLA:U_MLA + MLA_Q_LORA] = base + np.arange(MLA_Q_LORA)
    src[U_MLA + 256:U_MLA + 384] = base + MLA_Q_LORA + np.arange(MLA_KV_LORA)
    kr = base + MLA_Q_LORA + MLA_KV_LORA + np.arange(MLA_ROPE)
    for h in range(2):
        off = U_MLA + 384 + h * MLA_HEAD_PAD + MLA_NOPE
        src[off:off + MLA_ROPE] = kr
    return src


def _take_cols(w, src):
    parts = []
    start = 0
    for i in range(1, len(src) + 1):
        run_ends = (i == len(src) or (src[i] < 0) != (src[start] < 0)
                    or (src[start] >= 0 and src[i] != src[i - 1] + 1))
        if run_ends:
            if src[start] < 0:
                parts.append(jnp.zeros(w.shape[:-1] + (i - start,), w.dtype))
            else:
                parts.append(w[..., int(src[start]):int(src[start]) + i - start])
            start = i
    return jnp.concatenate(parts, axis=-1)


def _mla_uq_sources():
    src = np.full((512,), -1, np.int64)
    src[0:192] = np.arange(192)
    src[256:448] = 192 + np.arange(192)
    return src


def _mla_ukv_sources():
    src = np.full((768,), -1, np.int64)
    for h in range(4):
        off = (h // 2) * 256 + (h % 2) * MLA_HEAD_PAD
        src[off:off + MLA_NOPE] = h * 128 + np.arange(64)
        src[512 + h * 64:512 + (h + 1) * 64] = h * 128 + 64 + np.arange(64)
    return src


def _rope_tables(seq, ctx, rot_dim, lane_slots, width):
    n_freq = rot_dim // 4
    half = rot_dim // 2
    freq = np.zeros((width,), np.int64)
    use_row = np.zeros((width,), np.float32)
    use_col = np.zeros((width,), np.float32)
    sign = np.zeros((width,), np.float32)
    for off in lane_slots:
        for o in range(rot_dim):
            i = o % half
            freq[off + o] = i % n_freq
            use_row[off + o] = 1.0 if i < n_freq else 0.0
            use_col[off + o] = 0.0 if i < n_freq else 1.0
            sign[off + o] = -1.0 if o < half else 1.0
    inv = ROPE_BASE ** (-jnp.arange(n_freq, dtype=F32) / n_freq)
    inv_lane = jnp.take(inv, jnp.asarray(freq))
    ang_r = jnp.arange(seq // GRID_W, dtype=F32)[:, None] * (inv_lane * use_row)[None, :]
    ang_c = jnp.arange(GRID_W, dtype=F32)[:, None] * (inv_lane * use_col)[None, :]
    cos = (jnp.cos(ang_r)[:, None, :] * jnp.cos(ang_c)[None, :, :]).reshape(seq, width)
    sin = (jnp.sin(ang_r)[:, None, :] + jnp.sin(ang_c)[None, :, :]).reshape(seq, width) * sign[None, :]
    return (jnp.concatenate([jnp.ones((ctx, width), F32), cos], axis=0),
            jnp.concatenate([jnp.zeros((ctx, width), F32), sin], axis=0))


def _split3(v):
    hi = v.astype(BF16)
    r1 = v - hi.astype(F32)
    mid = r1.astype(BF16)
    lo = (r1 - mid.astype(F32)).astype(BF16)
    return hi, mid, lo


def _dot(a, b):
    return jnp.dot(a, b, preferred_element_type=F32)


def _dot_sel_rhs2(v, sel):
    hi = v.astype(BF16)
    lo = (v - hi.astype(F32)).astype(BF16)
    return _dot(hi, sel) + _dot(lo, sel)


def _dot_sel_lhs(sel, v):
    hi, mid, lo = _split3(v)
    return _dot(sel, hi) + _dot(sel, mid) + _dot(sel, lo)


def _iota(shape, axis):
    return lax.broadcasted_iota(jnp.int32, shape, axis)


def _seg_mean_matrix(width, seg):
    r = _iota((width, width), 0) // seg
    c = _iota((width, width), 1) // seg
    return jnp.where(r == c, 1.0 / seg, 0.0).astype(BF16)


def _rms_rows(x, g, n):
    ms = jnp.sum(x * x, axis=-1, keepdims=True) * (1.0 / n)
    return x * lax.rsqrt(ms + EPS) * g


def _rms_segments(x, g, seg):
    ms = _dot_sel_rhs2(x * x, _seg_mean_matrix(x.shape[-1], seg))
    return x * lax.rsqrt(ms + EPS) * g


def _rope(x, cos, sin, rot_dim):
    half = rot_dim // 2
    outs = []
    for s in range(x.shape[-1] // LANES):
        xs = x[:, s * LANES:(s + 1) * LANES]
        up = pltpu.roll(xs, LANES - half, axis=1)
        dn = pltpu.roll(xs, half, axis=1)
        first = (_iota(xs.shape, 1) % rot_dim) < half
        rot = jnp.where(first, up, dn)
        outs.append(xs * cos[:, s * LANES:(s + 1) * LANES] + rot * sin[:, s * LANES:(s + 1) * LANES])
    return jnp.concatenate(outs, axis=-1) if len(outs) > 1 else outs[0]


def _token_tile(refs, k, tile, n_ctx_tiles):
    if len(refs) == 1:
        return refs[0][0, k * ROW_TILE:(k + 1) * ROW_TILE, :]
    ctx_ref, x_ref = refs[2 * k], refs[2 * k + 1]
    is_ctx = jnp.full((ROW_TILE, 1), tile, jnp.int32) < n_ctx_tiles
    return jnp.where(is_ctx, ctx_ref[0], x_ref[0])


def _token_specs(streams, step_rows, first_tile, n_ctx_tiles):
    if len(streams) == 1:
        off = first_tile // (step_rows // ROW_TILE)
        return [pl.BlockSpec((1, step_rows, streams[0].shape[-1]), lambda i, bb: (bb, i + off, 0))]
    n_sub = step_rows // ROW_TILE
    ctx, x = streams
    d = x.shape[-1]
    specs = []
    for k in range(n_sub):
        tile = lambda i, k=k: first_tile + i * n_sub + k
        specs.append(pl.BlockSpec((1, ROW_TILE, d),
                                  lambda i, bb, tile=tile: (bb, jnp.minimum(tile(i), n_ctx_tiles - 1), 0)))
        specs.append(pl.BlockSpec((1, ROW_TILE, d),
                                  lambda i, bb, tile=tile: (bb, jnp.clip(tile(i) - n_ctx_tiles, 0,
                                                                        x.shape[1] // ROW_TILE - 1), 0)))
    return specs


def _with_ones_block(v):
    return jnp.concatenate([v[:, 0:LANES], jnp.ones((v.shape[0], LANES), v.dtype), v[:, LANES:2 * LANES]], axis=-1)


def _loop_unrolled(n, body, unroll=2):
    if n % unroll:
        unroll = 1

    def group(i, carry):
        for k in range(unroll):
            body(i * unroll + k, carry)
        return carry

    lax.fori_loop(0, n // unroll, group, 0)


def _silu(x):
    return x / (1.0 + jnp.exp(-x))


def _softplus(x):
    return jnp.maximum(x, 0.0) + jnp.log(1.0 + jnp.exp(-jnp.abs(x)))


def _mod_kernel(c_ref, w_ref, b_ref, o_ref):
    cond = _silu(c_ref[...]).astype(BF16)
    o_ref[0] = _dot(cond, w_ref[0].astype(BF16)) + b_ref[0]


def _modulation(cond_rows, mod_w, mod_b):
    depth, d, d6 = mod_w.shape
    rows = cond_rows.shape[0]
    return pl.pallas_call(
        _mod_kernel,
        grid=(depth, d6 // d),
        in_specs=[pl.BlockSpec((rows, d), lambda i, j: (0, 0)),
                  pl.BlockSpec((1, d, d), lambda i, j: (i, 0, j)),
                  pl.BlockSpec((1, 1, d), lambda i, j: (i, 0, j))],
        out_specs=pl.BlockSpec((1, rows, d), lambda i, j: (i, 0, j)),
        out_shape=jax.ShapeDtypeStruct((depth, rows, d6), F32),
        compiler_params=pltpu.CompilerParams(dimension_semantics=("parallel", "parallel"),
                                             vmem_limit_bytes=VMEM_LIMIT),
        name="modulation",
    )(cond_rows, mod_w, mod_b.reshape(depth, 1, d6))


def _in_kernel(*refs, n_ctx_tiles, n_tokens):
    tok_refs = refs[:n_tokens]
    (shb_ref, scb_ref, shc_ref, scc_ref, g_ref, w_ref, wuq_ref, wukv_ref,
     gq_ref, gk_ref, mq_ref, mkv_ref, cd_ref, sd_ref, cg_ref, sg_ref, cm_ref, sm_ref,
     ussd_ref, qd_ref, kd_ref, vd_ref, qg_ref, kg_ref, vg_ref, qm_ref, km_ref, vm_ref) = refs[n_tokens:]
    d_model = w_ref.shape[0]
    n_sub = ussd_ref.shape[1] // ROW_TILE
    for k in range(n_sub):
        rows = slice(k * ROW_TILE, (k + 1) * ROW_TILE)
        tile = pl.program_id(0) * n_sub + k
        is_ctx = jnp.where(tile < n_ctx_tiles, 1.0, 0.0)
        shift = is_ctx * shc_ref[0] + (1.0 - is_ctx) * shb_ref[0]
        scale = is_ctx * scc_ref[0] + (1.0 - is_ctx) * scb_ref[0]
        xn = _rms_rows(_token_tile(tok_refs, k, tile, n_ctx_tiles), g_ref[...], d_model) * (1.0 + scale) + shift
        u = _dot(xn.astype(BF16), w_ref[...])
        ussd_ref[0, rows, :] = u[:, U_SSD:U_SSD + U_SSD_W]

        q = _rope(u[:, U_DIFF:U_DIFF + 256], cd_ref[rows, :], sd_ref[rows, :], DIFF_QK)
        kk = _rope(u[:, U_DIFF + 256:U_DIFF + 512], cd_ref[rows, :], sd_ref[rows, :], DIFF_QK)
        qd_ref[0, 0, rows, :] = (q * (DIFF_QK ** -0.5 * LOG2E)).astype(BF16)
        kd_ref[0, 0, rows, :] = kk.astype(BF16)
        vd_ref[0, rows, :] = _with_ones_block(u[:, U_DIFF + 512:U_DIFF + 768].astype(BF16))

        q = _rms_segments(u[:, U_GQA:U_GQA + 256], gq_ref[...], HEAD_DIM)
        kk = _rms_segments(u[:, U_GQA + 256:U_GQA + 512], gk_ref[...], HEAD_DIM)
        q = _rope(q, cg_ref[rows, :], sg_ref[rows, :], HEAD_DIM)
        kk = _rope(kk, cg_ref[rows, :], sg_ref[rows, :], HEAD_DIM)
        qg_ref[0, 0, rows, :] = (q * (HEAD_DIM ** -0.5 * LOG2E)).astype(BF16)
        kg_ref[0, 0, rows, :] = kk.astype(BF16)
        vg_ref[0, rows, :] = _with_ones_block(u[:, U_GQA + 512:U_GQA + 768].astype(BF16))

        cq = _rms_rows(u[:, U_MLA:U_MLA + 256], mq_ref[...], MLA_Q_LORA)
        q = _rope(_dot(cq.astype(BF16), wuq_ref[...]), cm_ref[rows, :], sm_ref[rows, :], MLA_ROPE)
        q = (q * ((MLA_NOPE + MLA_ROPE) ** -0.5 * LOG2E)).astype(BF16)
        qm_ref[0, 0, rows, :] = q[:, 0:GROUP]
        qm_ref[0, 1, rows, :] = q[:, GROUP:2 * GROUP]
        ckv = _rms_rows(u[:, U_MLA + 256:U_MLA + 384], mkv_ref[...], MLA_KV_LORA)
        kv = _dot(ckv.astype(BF16), wukv_ref[...])
        kr = _rope(u[:, U_MLA + 384:U_MLA + 640], cm_ref[rows, 0:GROUP], sm_ref[rows, 0:GROUP], MLA_ROPE)
        km_ref[0, 0, rows, :] = (kv[:, 0:GROUP] + kr).astype(BF16)
        km_ref[0, 1, rows, :] = (kv[:, GROUP:2 * GROUP] + kr).astype(BF16)
        vm_ref[0, rows, :] = _with_ones_block(kv[:, 512:768].astype(BF16))


def _in_projection(streams, mod3, g1, w_in, w_uq, w_ukv, gq, gk, mq, mkv, tables, *, layer, ctx, batch):
    b, d = streams[0].shape[0], streams[0].shape[-1]
    t = sum(a.shape[1] for a in streams)
    step_rows = next(r for r in (3 * ROW_TILE, 2 * ROW_TILE, ROW_TILE) if t % r == 0)
    nt = t // step_rows

    row_spec = lambda w: pl.BlockSpec((1, step_rows, w), lambda i, bb: (bb, i, 0))
    tab_spec = lambda w: pl.BlockSpec((step_rows, w), lambda i, bb: (i, 0))
    const2 = lambda r, c: pl.BlockSpec((r, c), lambda i, bb: (0, 0))
    layer_w = lambda r, c: pl.BlockSpec((None, r, c), lambda i, bb: (layer, 0, 0))
    in_specs = _token_specs(streams, step_rows, 0, ctx // ROW_TILE) + [
        pl.BlockSpec((1, 1, d), lambda i, bb: (bb, 0, 0)),
        pl.BlockSpec((1, 1, d), lambda i, bb: (bb, 0, 1)),
        pl.BlockSpec((1, 1, d), lambda i, bb: (batch, 0, 0)),
        pl.BlockSpec((1, 1, d), lambda i, bb: (batch, 0, 1)),
        const2(1, d), layer_w(d, U_COLS), layer_w(256, 512), layer_w(128, 768),
        const2(1, 256), const2(1, 256), const2(1, 256), const2(1, 128),
        tab_spec(256), tab_spec(256), tab_spec(256), tab_spec(256), tab_spec(512), tab_spec(512),
    ]
    qk_spec = lambda n: pl.BlockSpec((1, n, step_rows, GROUP), lambda i, bb: (bb, 0, i, 0))
    qk_shape = lambda n: jax.ShapeDtypeStruct((b, n, t, GROUP), BF16)
    v_shape = jax.ShapeDtypeStruct((b, t, V_ONES_W), BF16)
    return pl.pallas_call(
        functools.partial(_in_kernel, n_ctx_tiles=ctx // ROW_TILE,
                          n_tokens=1 if len(streams) == 1 else 2 * (step_rows // ROW_TILE)),
        grid=(nt, b),
        in_specs=in_specs,
        out_specs=[row_spec(U_SSD_W), qk_spec(1), qk_spec(1), row_spec(V_ONES_W), qk_spec(1), qk_spec(1),
                   row_spec(V_ONES_W), qk_spec(2), qk_spec(2), row_spec(V_ONES_W)],
        out_shape=[jax.ShapeDtypeStruct((b, t, U_SSD_W), F32), qk_shape(1), qk_shape(1), v_shape,
                   qk_shape(1), qk_shape(1), v_shape, qk_shape(2), qk_shape(2), v_shape],
        compiler_params=pltpu.CompilerParams(dimension_semantics=("parallel", "parallel"),
                                             vmem_limit_bytes=VMEM_LIMIT),
        name="in_projection",
    )(*(streams if len(streams) == 1 else streams * (step_rows // ROW_TILE)),
      mod3, mod3, mod3, mod3, g1, w_in, w_uq, w_ukv, gq, gk, mq, mkv, *tables)


def _ssd_kernel(u_ref, cw_ref, cb_ref, dtb_ref, alog_ref, dsk_ref, ng_ref, y_ref,
                xs_s, c_s, yf_s, yb_s, upd_s, win_s, st_s, *, t_rows, ctx):
    L = SSD_CHUNK
    nc = t_rows // L
    ncc = ctx // L
    n_cols = 2 * SSD_HEADS
    row = _iota((L, SSD_CONV_CH), 0)
    lane = _iota((1, LANES), 1)
    a_neg = jnp.where(lane < n_cols, -jnp.exp(alog_ref[...]), 0.0)
    r128 = _iota((L, L), 0)
    c128 = _iota((L, L), 1)
    lane_l = _iota((L, LANES), 1)
    lane256 = _iota((L, GROUP), 1)
    causal = (r128 >= c128, r128 <= c128)
    tri = jnp.where(causal[0], 1.0, 0.0).astype(BF16)
    bd_mask = (_iota((L, GROUP), 0) // SSD_STATE) == (lane256 // (2 * HEAD_DIM))
    group_lanes = [c128 // SSD_STATE == g for g in range(2)]
    head_lanes = [lane256 // HEAD_DIM == h for h in range(SSD_HEADS)]
    sel = [jnp.where(_iota((LANES, GROUP), 0) == d * SSD_HEADS + _iota((LANES, GROUP), 1) // HEAD_DIM,
                     1.0, 0.0).astype(BF16) for d in range(2)]

    def chunk_body(c, carry):
        start = pl.multiple_of(c * L, L)
        x = u_ref[0, pl.ds(start, L), 256:768]
        first = jnp.logical_or(c == 0, c == ncc)
        last = jnp.logical_or(c == ncc - 1, c == nc - 1)
        prev8 = u_ref[0, pl.ds(pl.multiple_of(jnp.maximum(start - 8, 0), 8), 8), 256:768]
        next8 = u_ref[0, pl.ds(pl.multiple_of(jnp.minimum(start + L, t_rows - 8), 8), 8), 256:768]
        prev_row = prev8[7:8, :] * jnp.where(first, 0.0, 1.0)
        next_row = next8[0:1, :] * jnp.where(last, 0.0, 1.0)
        xp = jnp.where(row == 0, prev_row, pltpu.roll(x, 1, axis=0))
        xn = jnp.where(row == L - 1, next_row, pltpu.roll(x, L - 1, axis=0))
        xbc = _silu(cw_ref[0:1, :] * xp + cw_ref[1:2, :] * x + cw_ref[2:3, :] * xn + cb_ref[...])
        xs, bm, cm = xbc[:, 0:256], xbc[:, 256:384], xbc[:, 384:512]
        dtp = _softplus(u_ref[0, pl.ds(start, L), 768:896] + dtb_ref[...])

        a = dtp * a_neg
        cs_f = _dot_sel_lhs(tri, a)
        tot = cs_f[L - 1:L, :]
        cs = jnp.where(lane_l < SSD_HEADS, cs_f, tot - cs_f + a)
        in_dec = jnp.exp(cs)
        w_end = dtp * jnp.exp(tot - cs)
        rows_t = jnp.where(lane_l < n_cols, cs, dtp).T

        xs16 = xs.astype(BF16)
        cb16 = cm.astype(BF16)
        bt16 = bm.T.astype(BF16)
        gram = [_dot(jnp.where(group_lanes[g], cb16, jnp.zeros_like(cb16)), bt16) for g in range(2)]
        x_heads = jnp.concatenate([jnp.where(head_lanes[h], xs16, jnp.zeros_like(xs16))
                                   for h in range(SSD_HEADS)], axis=0)
        for d, y_dst in ((0, yf_s), (1, yb_s)):
            mixes = []
            for h in range(SSD_HEADS):
                j = d * SSD_HEADS + h
                delta = cs[:, j:j + 1] - rows_t[j:j + 1, :]
                decay = jnp.exp(jnp.where(causal[d], delta, NEG_BIG))
                mixes.append((gram[h // 2] * decay * rows_t[n_cols + j:n_cols + j + 1, :]).astype(BF16))
            y_dst[pl.ds(start, L), :] = _dot(jnp.concatenate(mixes, axis=1), x_heads)
            win_s[d, pl.ds(start, L), :] = _dot_sel_rhs2(in_dec, sel[d])
            x_st = (xs * _dot_sel_rhs2(w_end, sel[d])).astype(BF16)
            upd_s[d, pl.ds(start, L), :] = jnp.where(bd_mask, _dot(bt16, x_st), 0.0)
        xs_s[pl.ds(start, L), :] = xs
        c_s[pl.ds(start, L), :] = cb16
        return carry

    _loop_unrolled(nc, chunk_body)

    st_s[...] = jnp.zeros_like(st_s)

    def carry_state(c, d, y_dst):
        start = pl.multiple_of(c * L, L)
        state = st_s[d]
        w_in = win_s[d, pl.ds(start, L), :]
        y_dst[pl.ds(start, L), :] += _dot(c_s[pl.ds(start, L), :], state.astype(BF16)) * w_in
        c_dec = w_in[L - 1:L, :] if d == 0 else w_in[0:1, :]
        st_s[d] = state * c_dec + upd_s[d, pl.ds(start, L), :]

    def scan_body(i, carry):
        carry_state(i, 0, yf_s)
        carry_state(jnp.where(i < ncc, ncc - 1 - i, nc - 1 - (i - ncc)), 1, yb_s)
        return carry

    _loop_unrolled(nc, scan_body)

    d_sum = dsk_ref[0:1, :] + dsk_ref[1:2, :]

    def finish_body(c, carry):
        start = pl.multiple_of(c * L, L)
        y = yf_s[pl.ds(start, L), :] + yb_s[pl.ds(start, L), :] + d_sum * xs_s[pl.ds(start, L), :]
        gated = y * _silu(u_ref[0, pl.ds(start, L), 0:256])
        y_ref[0, pl.ds(start, L), :] = _rms_rows(gated, ng_ref[...], GROUP).astype(y_ref.dtype)
        return carry

    _loop_unrolled(nc, finish_body)


def _ssd(u_ssd, conv_w, conv_b, dt_bias, a_log, d_skip, norm_g, *, ctx):
    b, t, w = u_ssd.shape
    const2 = lambda r, c: pl.BlockSpec((r, c), lambda bb: (0, 0))
    return pl.pallas_call(
        functools.partial(_ssd_kernel, t_rows=t, ctx=ctx),
        grid=(b,),
        in_specs=[pl.BlockSpec((1, t, w), lambda bb: (bb, 0, 0)),
                  const2(3, SSD_CONV_CH), const2(1, SSD_CONV_CH), const2(1, LANES), const2(1, LANES),
                  const2(2, GROUP), const2(1, GROUP)],
        out_specs=pl.BlockSpec((1, t, GROUP), lambda bb: (bb, 0, 0)),
        out_shape=jax.ShapeDtypeStruct((b, t, GROUP), BF16),
        scratch_shapes=[pltpu.VMEM((t, GROUP), F32), pltpu.VMEM((t, LANES), BF16),
                        pltpu.VMEM((t, GROUP), F32), pltpu.VMEM((t, GROUP), F32),
                        pltpu.VMEM((2, t, GROUP), F32), pltpu.VMEM((2, t, GROUP), F32),
                        pltpu.VMEM((2, SSD_CHUNK, GROUP), F32)],
        compiler_params=pltpu.CompilerParams(dimension_semantics=("parallel",),
                                             vmem_limit_bytes=VMEM_LIMIT),
        name="ssd_mixer",
    )(u_ssd, conv_w, conv_b, dt_bias, a_log, d_skip, norm_g)


class AttnCfg(NamedTuple):
    heads: int
    maps: int
    heads_per_block: int
    head_stride: int
    map_width: int


DIFF_CFG = AttnCfg(heads=4, maps=2, heads_per_block=4, head_stride=64, map_width=DIFF_QK)
GQA_CFG = AttnCfg(heads=4, maps=1, heads_per_block=4, head_stride=64, map_width=HEAD_DIM)
MLA_CFG = AttnCfg(heads=4, maps=1, heads_per_block=2, head_stride=MLA_HEAD_PAD, map_width=MLA_HEAD_PAD)


class _AttnGroup:
    def __init__(self, cfg, diff_scale, refs_in, o_ref, scratch, ctx, with_ctx, tq):
        self.cfg, self.diff_scale, self.ctx, self.tq = cfg, diff_scale, ctx, tq
        self.q_ref, self.k_ref, self.v_ref = refs_in[:3]
        self.o_ref = o_ref
        self.s_scr, self.m_scr = scratch[0:2], scratch[2:4]
        self.out_off = ctx if with_ctx else 0
        assert cfg.heads % 2 == 0
        if diff_scale is not None:
            lam_ref, self.ng_ref = refs_in[3:5]
            lp = lam_ref[...]
            self.lam = (jnp.exp(jnp.sum(lp[0:1] * lp[1:2], axis=-1, keepdims=True))
                        - jnp.exp(jnp.sum(lp[2:3] * lp[3:4], axis=-1, keepdims=True)) + diff_scale[0])

    @staticmethod
    def lane_range(lo, width, shape):
        lanes = _iota(shape, 1)
        return jnp.logical_and(lanes >= lo, lanes < lo + width)

    def scores(self, qb, kb, h):
        cfg, out = self.cfg, []
        for j in range(cfg.maps):
            sel = self.lane_range(h * cfg.head_stride + j * cfg.map_width, cfg.map_width, qb.shape)
            qm = jnp.where(sel, qb, jnp.zeros_like(qb))
            s = lax.dot_general(qm, kb, (((1,), (1,)), ((), ())), preferred_element_type=F32)
            out.append((s, jnp.max(s, axis=-1, keepdims=True)))
        return out

    def head_output(self, sm, v, h):
        lo = 0 if h < self.cfg.heads // 2 else LANES
        ones_lane = LANES - lo
        outs = []
        for s, m in sm:
            o = _dot(jnp.exp2(s - m).astype(BF16), v[:, lo:lo + GROUP])
            outs.append(o * (1.0 / o[:, ones_lane:ones_lane + 1]))
        o = outs[0] if self.cfg.maps == 1 else outs[0] - self.lam * outs[1]
        return jnp.where(self.lane_range(h * HEAD_DIM, HEAD_DIM, o.shape), o, 0.0)

    def issue_scores(self, qi, h, slot):
        blk, h_in = divmod(h, self.cfg.heads_per_block)
        row0 = pl.multiple_of(self.ctx + qi * self.tq, self.tq)
        sm = self.scores(self.q_ref[0, blk, pl.ds(row0, self.tq), :], self.k_ref[0, blk], h_in)
        for j, (s, m) in enumerate(sm):
            self.s_scr[slot][j] = s
            self.m_scr[slot][j] = m

    def slot_output(self, h):
        slot = h % 2
        sm = [(self.s_scr[slot][j], self.m_scr[slot][j]) for j in range(self.cfg.maps)]
        return self.head_output(sm, self.v_ref[0], h)

    def store(self, rows, acc):
        if self.diff_scale is not None:
            acc = _rms_segments(acc, self.ng_ref[...], HEAD_DIM) * self.diff_scale[1]
        self.o_ref[0, rows, :] = acc.astype(self.o_ref.dtype)

    def context_block(self):
        ctx = self.ctx
        acc = jnp.zeros((ctx, GROUP), F32)
        for h in range(self.cfg.heads):
            blk, h_in = divmod(h, self.cfg.heads_per_block)
            sm = self.scores(self.q_ref[0, blk, 0:ctx, :], self.k_ref[0, blk, 0:ctx, :], h_in)
            acc = acc + self.head_output(sm, self.v_ref[0, 0:ctx, :], h)
        self.store(slice(0, ctx), acc)


def _attn_kernel(*refs, groups, ctx, with_ctx, tq):
    n_in = [3 if ds is None else 5 for _, ds in groups]
    ins = [refs[sum(n_in[:i]):sum(n_in[:i + 1])] for i in range(len(groups))]
    outs = refs[sum(n_in):sum(n_in) + len(groups)]
    scr = refs[sum(n_in) + len(groups):]
    gs = [_AttnGroup(cfg, ds, ins[i], outs[i], scr[4 * i:4 * i + 4], ctx, with_ctx, tq)
          for i, (cfg, ds) in enumerate(groups)]
    n_heads = groups[0][0].heads
    assert all(cfg.heads == n_heads for cfg, _ in groups)
    n_blocks = (gs[0].k_ref.shape[2] - ctx) // tq

    def block_body(qi, carry, last=False):
        accs = [jnp.zeros((tq, GROUP), F32) for _ in gs]
        for h in range(n_heads):
            for g in gs:
                if h + 1 < n_heads:
                    g.issue_scores(qi, h + 1, (h + 1) % 2)
                elif not last:
                    g.issue_scores(qi + 1, 0, 0)
            accs = [acc + g.slot_output(h) for g, acc in zip(gs, accs)]
        for g, acc in zip(gs, accs):
            g.store(pl.ds(pl.multiple_of(g.out_off + qi * tq, tq), tq), acc)
        return carry

    for g in gs:
        g.issue_scores(0, 0, 0)
    lax.fori_loop(0, n_blocks - 1, block_body, 0)
    block_body(n_blocks - 1, 0, last=True)
    if with_ctx:
        for g in gs:
            g.context_block()


def _attention(inputs, *, groups, ctx, with_ctx, name):
    b, _, t, _ = inputs[0][0].shape
    tq = ATTN_Q_TILE
    assert (t - ctx) % tq == 0
    out_rows = t if with_ctx else t - ctx
    in_specs, args, scratch = [], [], []
    for (q, k, v, *extra), (cfg, _) in zip(inputs, groups):
        qk_spec = pl.BlockSpec((1, q.shape[1], t, GROUP), lambda bb: (bb, 0, 0, 0))
        in_specs += [qk_spec, qk_spec, pl.BlockSpec((1, t, v.shape[-1]), lambda bb: (bb, 0, 0))]
        in_specs += [pl.BlockSpec(e.shape, lambda bb: (0, 0)) for e in extra]
        args += [q, k, v, *extra]
        scratch += [pltpu.VMEM((cfg.maps, tq, t), F32)] * 2 + [pltpu.VMEM((cfg.maps, tq, 1), F32)] * 2
    return pl.pallas_call(
        functools.partial(_attn_kernel, groups=groups, ctx=ctx, with_ctx=with_ctx, tq=tq),
        grid=(b,),
        in_specs=in_specs,
        out_specs=[pl.BlockSpec((1, out_rows, GROUP), lambda bb: (bb, 0, 0))] * len(groups),
        out_shape=[jax.ShapeDtypeStruct((b, out_rows, GROUP), BF16)] * len(groups),
        scratch_shapes=scratch,
        compiler_params=pltpu.CompilerParams(dimension_semantics=("parallel",),
                                             vmem_limit_bytes=VMEM_LIMIT),
        name=name,
    )(*args)


def _out_kernel(*refs, final, first_tile, n_ctx_tiles, n_tokens):
    tok_refs = refs[:n_tokens]
    (ya_ref, yb_ref, yc_ref, yd_ref, modb_ref, modc_ref, n2_ref,
     wo_ref, w1_ref, w2_ref, fg_ref, o_ref) = refs[n_tokens:]
    d_model = o_ref.shape[-1]
    n_sub = o_ref.shape[1] // ROW_TILE
    for k in range(n_sub):
        rows = slice(k * ROW_TILE, (k + 1) * ROW_TILE)
        tile = first_tile + pl.program_id(0) * n_sub + k
        is_ctx = jnp.where(tile < n_ctx_tiles, 1.0, 0.0)
        mod = is_ctx * modc_ref[0] + (1.0 - is_ctx) * modb_ref[0]
        gate1, shift2, scale2, gate2 = (mod[:, i * d_model:(i + 1) * d_model] for i in range(2, 6))
        ycat = jnp.concatenate([ya_ref[0, rows, :], yb_ref[0, rows, :], yc_ref[0, rows, :], yd_ref[0, rows, :]],
                               axis=-1)
        h1 = _token_tile(tok_refs, k, tile, n_ctx_tiles) + gate1 * _dot(ycat, wo_ref[...])
        t = _rms_rows(h1, n2_ref[...], d_model) * (1.0 + scale2) + shift2
        mid = jnp.maximum(_dot(t.astype(BF16), w1_ref[...]), 0.0)
        h2 = h1 + gate2 * _dot((mid * mid).astype(BF16), w2_ref[...])
        if final:
            h2 = _rms_rows(h2, fg_ref[...], d_model)
        o_ref[0, rows, :] = h2


def _out_block(streams, ys, mod3, n2, w_out, w1, w2, fg, *, layer, ctx, batch, final):
    b, d = streams[0].shape[0], streams[0].shape[-1]
    t = sum(a.shape[1] for a in streams)
    d_ff = w1.shape[-1]
    out_rows = t - ctx if final else t
    step_rows = next(r for r in (3 * ROW_TILE, 2 * ROW_TILE, ROW_TILE)
                     if out_rows % r == 0 and (not final or ctx % r == 0))
    off = ctx // step_rows if final else 0
    nt = out_rows // step_rows

    row_spec = lambda w, o: pl.BlockSpec((1, step_rows, w), lambda i, bb: (bb, i + o, 0))
    mod_spec = lambda r: pl.BlockSpec((1, 1, 6 * d), lambda i, bb: (bb if r is None else r, 0, 0))
    const2 = lambda r, c: pl.BlockSpec((r, c), lambda i, bb: (0, 0), pipeline_mode=pl.Buffered(1))
    layer_w = lambda r, c: pl.BlockSpec((None, r, c), lambda i, bb: (layer, 0, 0), pipeline_mode=pl.Buffered(1))
    return pl.pallas_call(
        functools.partial(_out_kernel, final=final, first_tile=off * (step_rows // ROW_TILE),
                          n_ctx_tiles=ctx // ROW_TILE,
                          n_tokens=1 if len(streams) == 1 else 2 * (step_rows // ROW_TILE)),
        grid=(nt, b),
        in_specs=_token_specs(streams, step_rows, off * (step_rows // ROW_TILE), ctx // ROW_TILE)
                 + [row_spec(GROUP, off)] + [row_spec(GROUP, 0)] * 3
                 + [mod_spec(None), mod_spec(batch)]
                 + [const2(1, d), layer_w(d, d), layer_w(d, d_ff), layer_w(d_ff, d), const2(1, d)],
        out_specs=pl.BlockSpec((1, step_rows, d), lambda i, bb: (bb, i, 0)),
        out_shape=jax.ShapeDtypeStruct((b, out_rows, d), F32),
        compiler_params=pltpu.CompilerParams(dimension_semantics=("parallel", "parallel"),
                                             vmem_limit_bytes=VMEM_LIMIT),
        name="out_block",
    )(*(streams if len(streams) == 1 else streams * (step_rows // ROW_TILE)),
      *ys, mod3, mod3, n2, w_out, w1, w2, fg)


def kernel(x, c, ctx, c_ctx, mod_w, mod_b, norm1_g, norm2_g, w_in, ssd_conv_w, ssd_conv_b, ssd_dt_bias,
           ssd_a_log, ssd_d, ssd_norm_g, diff_lambda, diff_norm_g, gqa_q_norm, gqa_k_norm, mla_q_norm,
           mla_kv_norm, mla_w_uq, mla_w_ukv, w_out, mlp_w1, mlp_w2, final_norm_g):
    batch, seq, d_model = x.shape
    n_ctx = ctx.shape[1]
    depth = mod_w.shape[0]
    assert n_ctx % ROW_TILE == 0 and seq % ROW_TILE == 0 and seq % GRID_W == 0
    assert w_in.shape[-1] == 2408 and d_model == 4 * GROUP

    pad_rows = -(batch + 1) % 8
    cond_rows = jnp.concatenate([c, c_ctx[None, :], jnp.zeros((pad_rows, d_model), F32)], axis=0)
    mod_all = _modulation(cond_rows, mod_w, mod_b)

    w_in_p = _take_cols(w_in.astype(BF16), _in_col_sources())
    w_uq_p = jnp.pad(_take_cols(mla_w_uq.astype(BF16), _mla_uq_sources()), ((0, 0), (0, 256 - MLA_Q_LORA), (0, 0)))
    w_ukv_p = _take_cols(mla_w_ukv.astype(BF16), _mla_ukv_sources())
    w_out_b = w_out.astype(BF16)
    w1_b = mlp_w1.astype(BF16)
    w2_b = mlp_w2.astype(BF16)

    slots32 = [s * 32 for s in range(8)]
    slots64 = [s * 64 for s in range(4)]
    slots_mla = [blk * 256 + hh * MLA_HEAD_PAD + MLA_NOPE for blk in range(2) for hh in range(2)]
    tables = (*_rope_tables(seq, n_ctx, DIFF_QK, slots32, 256),
              *_rope_tables(seq, n_ctx, HEAD_DIM, slots64, 256),
              *_rope_tables(seq, n_ctx, MLA_ROPE, slots_mla, 512))

    tile4 = lambda g: jnp.tile(g, 4)[None, :]
    streams = (ctx, x)
    for i in range(depth):
        final = i == depth - 1
        with_ctx = not final
        lam_init = 0.8 - 0.6 * math.exp(-0.3 * i)
        mod3 = mod_all[i].reshape(mod_all.shape[1], 1, 6 * d_model)
        mq = jnp.pad(mla_q_norm[i], (0, 256 - MLA_Q_LORA))[None, :]
        (u_ssd, qd, kd, vd, qg, kg, vg, qm, km, vm) = _in_projection(
            streams, mod3, norm1_g[i][None, :], w_in_p, w_uq_p, w_ukv_p,
            tile4(gqa_q_norm[i]), tile4(gqa_k_norm[i]), mq, mla_kv_norm[i][None, :], tables,
            layer=i, ctx=n_ctx, batch=batch)

        pad8 = lambda v: jnp.pad(v.reshape(-1), (0, LANES - 2 * SSD_HEADS))[None, :]
        dt_bias2 = jnp.pad(jnp.tile(ssd_dt_bias[i].reshape(-1), 2), (0, LANES - 4 * SSD_HEADS))[None, :]
        y_a = _ssd(u_ssd, ssd_conv_w[i].T, ssd_conv_b[i][None, :], dt_bias2, pad8(ssd_a_log[i]),
                   jnp.repeat(ssd_d[i], HEAD_DIM, axis=-1), ssd_norm_g[i][None, :], ctx=n_ctx)
        (y_b,) = _attention([(qd, kd, vd, diff_lambda[i], tile4(diff_norm_g[i]))],
                            groups=((DIFF_CFG, (lam_init, 1.0 - lam_init)),),
                            ctx=n_ctx, with_ctx=with_ctx, name="diff_attention")
        y_c, y_d = _attention([(qg, kg, vg), (qm, km, vm)], groups=((GQA_CFG, None), (MLA_CFG, None)),
                              ctx=n_ctx, with_ctx=with_ctx, name="softmax_attention")
        streams = (_out_block(streams, (y_a, y_b, y_c, y_d), mod3, norm2_g[i][None, :], w_out_b, w1_b, w2_b,
                              final_norm_g[None, :], layer=i, ctx=n_ctx, batch=batch, final=final),)
    return streams[0]
```

```python
import functools
import math
from typing import NamedTuple

import jax
import jax.numpy as jnp
import numpy as np
from jax import lax
from jax.experimental import pallas as pl
from jax.experimental.pallas import tpu as pltpu

F32 = jnp.float32
BF16 = jnp.bfloat16

LANES = 128
ROW_TILE = 256
ATTN_Q_TILE = 256
VMEM_LIMIT = 56 * 1024 * 1024
WEIGHT_CHUNK_BYTES = 2 * 1024 * 1024

GRID_W = 64
ROPE_BASE = 10000.0
EPS = 1e-6
LOG2E = math.log2(math.e)
NEG_BIG = -1e30

HEAD_DIM = 64
GROUP = 256
V_ONES_W = GROUP + 128
SSD_CHUNK = 128
SSD_HEADS = 4
SSD_STATE = 64
SSD_CONV_CH = 512
DIFF_QK = 32
MLA_Q_LORA = 192
MLA_KV_LORA = 128
MLA_ROPE = 32
MLA_NOPE = 64
MLA_HEAD_PAD = 96

U_SSD = 0
U_SSD_W = 896
U_DIFF = U_SSD + U_SSD_W
U_GQA = U_DIFF + 768
U_MLA = U_GQA + 768
U_COLS = U_MLA + 640


def _in_col_sources():
    src = np.full((U_COLS,), -1, np.int64)
    src[0:776] = np.arange(776)
    src[776:784] = np.arange(768, 776)
    base = 776
    src[U_DIFF:U_DIFF + 768] = base + np.arange(768)
    base = 776 + 768
    src[U_GQA:U_GQA + 256] = base + np.arange(256)
    dup = np.concatenate([np.arange(64), np.arange(64), 64 + np.arange(64), 64 + np.arange(64)])
    src[U_GQA + 256:U_GQA + 512] = base + 256 + dup
    src[U_GQA + 512:U_GQA + 768] = base + 384 + dup
    base = 776 + 768 + 512
    src[U_MLA:U_MLA + MLA_Q_LORA] = base + np.arange(MLA_Q_LORA)
    src[U_MLA + 256:U_MLA + 384] = base + MLA_Q_LORA + np.arange(MLA_KV_LORA)
    kr = base + MLA_Q_LORA + MLA_KV_LORA + np.arange(MLA_ROPE)
    for h in range(2):
        off = U_MLA + 384 + h * MLA_HEAD_PAD + MLA_NOPE
        src[off:off + MLA_ROPE] = kr
    return src


def _take_cols(w, src):
    parts = []
    start = 0
    for i in range(1, len(src) + 1):
        run_ends = (i == len(src) or (src[i] < 0) != (src[start] < 0)
                    or (src[start] >= 0 and src[i] != src[i - 1] + 1))
        if run_ends:
            if src[start] < 0:
                parts.append(jnp.zeros(w.shape[:-1] + (i - start,), w.dtype))
            else:
                parts.append(w[..., int(src[start]):int(src[start]) + i - start])
            start = i
    return jnp.concatenate(parts, axis=-1)


def _mla_uq_sources():
    src = np.full((512,), -1, np.int64)
    src[0:192] = np.arange(192)
    src[256:448] = 192 + np.arange(192)
    return src


def _mla_ukv_sources():
    src = np.full((768,), -1, np.int64)
    for h in range(4):
        off = (h // 2) * 256 + (h % 2) * MLA_HEAD_PAD
        src[off:off + MLA_NOPE] = h * 128 + np.arange(64)
        src[512 + h * 64:512 + (h + 1) * 64] = h * 128 + 64 + np.arange(64)
    return src


def _rope_tables(seq, ctx, rot_dim, lane_slots, width):
    n_freq = rot_dim // 4
    half = rot_dim // 2
    freq = np.zeros((width,), np.int64)
    use_row = np.zeros((width,), np.float32)
    use_col = np.zeros((width,), np.float32)
    sign = np.zeros((width,), np.float32)
    for off in lane_slots:
        for o in range(rot_dim):
            i = o % half
            freq[off + o] = i % n_freq
            use_row[off + o] = 1.0 if i < n_freq else 0.0
            use_col[off + o] = 0.0 if i < n_freq else 1.0
            sign[off + o] = -1.0 if o < half else 1.0
    inv = ROPE_BASE ** (-jnp.arange(n_freq, dtype=F32) / n_freq)
    inv_lane = jnp.take(inv, jnp.asarray(freq))
    ang_r = jnp.arange(seq // GRID_W, dtype=F32)[:, None] * (inv_lane * use_row)[None, :]
    ang_c = jnp.arange(GRID_W, dtype=F32)[:, None] * (inv_lane * use_col)[None, :]
    cos = (jnp.cos(ang_r)[:, None, :] * jnp.cos(ang_c)[None, :, :]).reshape(seq, width)
    sin = (jnp.sin(ang_r)[:, None, :] + jnp.sin(ang_c)[None, :, :]).reshape(seq, width) * sign[None, :]
    return (jnp.concatenate([jnp.ones((ctx, width), F32), cos], axis=0),
            jnp.concatenate([jnp.zeros((ctx, width), F32), sin], axis=0))


def _split3(v):
    hi = v.astype(BF16)
    r1 = v - hi.astype(F32)
    mid = r1.astype(BF16)
    lo = (r1 - mid.astype(F32)).astype(BF16)
    return hi, mid, lo


def _dot(a, b):
    return jnp.dot(a, b, preferred_element_type=F32)


def _dot_sel_rhs2(v, sel):
    hi = v.astype(BF16)
    lo = (v - hi.astype(F32)).astype(BF16)
    return _dot(hi, sel) + _dot(lo, sel)


def _dot_sel_lhs(sel, v):
    hi, mid, lo = _split3(v)
    return _dot(sel, hi) + _dot(sel, mid) + _dot(sel, lo)


def _iota(shape, axis):
    return lax.broadcasted_iota(jnp.int32, shape, axis)


def _seg_mean_matrix(width, seg):
    r = _iota((width, width), 0) // seg
    c = _iota((width, width), 1) // seg
    return jnp.where(r == c, 1.0 / seg, 0.0).astype(BF16)


def _rms_rows(x, g, n):
    ms = jnp.sum(x * x, axis=-1, keepdims=True) * (1.0 / n)
    return x * lax.rsqrt(ms + EPS) * g


def _rms_segments(x, g, seg):
    ms = _dot_sel_rhs2(x * x, _seg_mean_matrix(x.shape[-1], seg))
    return x * lax.rsqrt(ms + EPS) * g


def _rope(x, cos, sin, rot_dim):
    half = rot_dim // 2
    outs = []
    for s in range(x.shape[-1] // LANES):
        xs = x[:, s * LANES:(s + 1) * LANES]
        up = pltpu.roll(xs, LANES - half, axis=1)
        dn = pltpu.roll(xs, half, axis=1)
        first = (_iota(xs.shape, 1) % rot_dim) < half
        rot = jnp.where(first, up, dn)
        outs.append(xs * cos[:, s * LANES:(s + 1) * LANES] + rot * sin[:, s * LANES:(s + 1) * LANES])
    return jnp.concatenate(outs, axis=-1) if len(outs) > 1 else outs[0]


def _token_tile(refs, k, tile, n_ctx_tiles):
    if len(refs) == 1:
        return refs[0][0, k * ROW_TILE:(k + 1) * ROW_TILE, :]
    ctx_ref, x_ref = refs[2 * k], refs[2 * k + 1]
    is_ctx = jnp.full((ROW_TILE, 1), tile, jnp.int32) < n_ctx_tiles
    return jnp.where(is_ctx, ctx_ref[0], x_ref[0])


def _token_specs(streams, step_rows, first_tile, n_ctx_tiles):
    if len(streams) == 1:
        off = first_tile // (step_rows // ROW_TILE)
        return [pl.BlockSpec((1, step_rows, streams[0].shape[-1]), lambda i, bb: (bb, i + off, 0))]
    n_sub = step_rows // ROW_TILE
    ctx, x = streams
    d = x.shape[-1]
    specs = []
    for k in range(n_sub):
        tile = lambda i, k=k: first_tile + i * n_sub + k
        specs.append(pl.BlockSpec((1, ROW_TILE, d),
                                  lambda i, bb, tile=tile: (bb, jnp.minimum(tile(i), n_ctx_tiles - 1), 0)))
        specs.append(pl.BlockSpec((1, ROW_TILE, d),
                                  lambda i, bb, tile=tile: (bb, jnp.clip(tile(i) - n_ctx_tiles, 0,
                                                                        x.shape[1] // ROW_TILE - 1), 0)))
    return specs


def _with_ones_block(v):
    return jnp.concatenate([v[:, 0:LANES], jnp.ones((v.shape[0], LANES), v.dtype), v[:, LANES:2 * LANES]], axis=-1)


def _loop_unrolled(n, body, unroll=2):
    if n % unroll:
        unroll = 1

    def group(i, carry):
        for k in range(unroll):
            body(i * unroll + k, carry)
        return carry

    lax.fori_loop(0, n // unroll, group, 0)


def _silu(x):
    return x / (1.0 + jnp.exp(-x))


def _softplus(x):
    return jnp.maximum(x, 0.0) + jnp.log(1.0 + jnp.exp(-jnp.abs(x)))


def _mod_kernel(c_ref, w_ref, b_ref, o_ref):
    cond = _silu(c_ref[...]).astype(BF16)
    o_ref[0] = _dot(cond, w_ref[0].astype(BF16)) + b_ref[0]


def _modulation(cond_rows, mod_w, mod_b):
    depth, d, d6 = mod_w.shape
    rows = cond_rows.shape[0]
    return pl.pallas_call(
        _mod_kernel,
        grid=(depth, d6 // d),
        in_specs=[pl.BlockSpec((rows, d), lambda i, j: (0, 0)),
                  pl.BlockSpec((1, d, d), lambda i, j: (i, 0, j)),
                  pl.BlockSpec((1, 1, d), lambda i, j: (i, 0, j))],
        out_specs=pl.BlockSpec((1, rows, d), lambda i, j: (i, 0, j)),
        out_shape=jax.ShapeDtypeStruct((depth, rows, d6), F32),
        compiler_params=pltpu.CompilerParams(dimension_semantics=("parallel", "parallel"),
                                             vmem_limit_bytes=VMEM_LIMIT),
        name="modulation",
    )(cond_rows, mod_w, mod_b.reshape(depth, 1, d6))


def _in_kernel(*refs, n_ctx_tiles, n_tokens):
    tok_refs = refs[:n_tokens]
    (shb_ref, scb_ref, shc_ref, scc_ref, g_ref, w_ref, wuq_ref, wukv_ref,
     gq_ref, gk_ref, mq_ref, mkv_ref, cd_ref, sd_ref, cg_ref, sg_ref, cm_ref, sm_ref,
     ussd_ref, qd_ref, kd_ref, vd_ref, qg_ref, kg_ref, vg_ref, qm_ref, km_ref, vm_ref) = refs[n_tokens:]
    d_model = w_ref.shape[0]
    n_sub = ussd_ref.shape[1] // ROW_TILE
    for k in range(n_sub):
        rows = slice(k * ROW_TILE, (k + 1) * ROW_TILE)
        tile = pl.program_id(0) * n_sub + k
        is_ctx = jnp.where(tile < n_ctx_tiles, 1.0, 0.0)
        shift = is_ctx * shc_ref[0] + (1.0 - is_ctx) * shb_ref[0]
        scale = is_ctx * scc_ref[0] + (1.0 - is_ctx) * scb_ref[0]
        xn = _rms_rows(_token_tile(tok_refs, k, tile, n_ctx_tiles), g_ref[...], d_model) * (1.0 + scale) + shift
        u = _dot(xn.astype(BF16), w_ref[...])
        ussd_ref[0, rows, :] = u[:, U_SSD:U_SSD + U_SSD_W]

        q = _rope(u[:, U_DIFF:U_DIFF + 256], cd_ref[rows, :], sd_ref[rows, :], DIFF_QK)
        kk = _rope(u[:, U_DIFF + 256:U_DIFF + 512], cd_ref[rows, :], sd_ref[rows, :], DIFF_QK)
        qd_ref[0, 0, rows, :] = (q * (DIFF_QK ** -0.5 * LOG2E)).astype(BF16)
        kd_ref[0, 0, rows, :] = kk.astype(BF16)
        vd_ref[0, rows, :] = _with_ones_block(u[:, U_DIFF + 512:U_DIFF + 768].astype(BF16))

        q = _rms_segments(u[:, U_GQA:U_GQA + 256], gq_ref[...], HEAD_DIM)
        kk = _rms_segments(u[:, U_GQA + 256:U_GQA + 512], gk_ref[...], HEAD_DIM)
        q = _rope(q, cg_ref[rows, :], sg_ref[rows, :], HEAD_DIM)
        kk = _rope(kk, cg_ref[rows, :], sg_ref[rows, :], HEAD_DIM)
        qg_ref[0, 0, rows, :] = (q * (HEAD_DIM ** -0.5 * LOG2E)).astype(BF16)
        kg_ref[0, 0, rows, :] = kk.astype(BF16)
        vg_ref[0, rows, :] = _with_ones_block(u[:, U_GQA + 512:U_GQA + 768].astype(BF16))

        cq = _rms_rows(u[:, U_MLA:U_MLA + 256], mq_ref[...], MLA_Q_LORA)
        q = _rope(_dot(cq.astype(BF16), wuq_ref[...]), cm_ref[rows, :], sm_ref[rows, :], MLA_ROPE)
        q = (q * ((MLA_NOPE + MLA_ROPE) ** -0.5 * LOG2E)).astype(BF16)
        qm_ref[0, 0, rows, :] = q[:, 0:GROUP]
        qm_ref[0, 1, rows, :] = q[:, GROUP:2 * GROUP]
        ckv = _rms_rows(u[:, U_MLA + 256:U_MLA + 384], mkv_ref[...], MLA_KV_LORA)
        kv = _dot(ckv.astype(BF16), wukv_ref[...])
        kr = _rope(u[:, U_MLA + 384:U_MLA + 640], cm_ref[rows, 0:GROUP], sm_ref[rows, 0:GROUP], MLA_ROPE)
        km_ref[0, 0, rows, :] = (kv[:, 0:GROUP] + kr).astype(BF16)
        km_ref[0, 1, rows, :] = (kv[:, GROUP:2 * GROUP] + kr).astype(BF16)
        vm_ref[0, rows, :] = _with_ones_block(kv[:, 512:768].astype(BF16))


def _in_projection(streams, mod3, g1, w_in, w_uq, w_ukv, gq, gk, mq, mkv, tables, *, layer, ctx, batch):
    b, d = streams[0].shape[0], streams[0].shape[-1]
    t = sum(a.shape[1] for a in streams)
    step_rows = next(r for r in (3 * ROW_TILE, 2 * ROW_TILE, ROW_TILE) if t % r == 0)
    nt = t // step_rows

    row_spec = lambda w: pl.BlockSpec((1, step_rows, w), lambda i, bb: (bb, i, 0))
    tab_spec = lambda w: pl.BlockSpec((step_rows, w), lambda i, bb: (i, 0))
    const2 = lambda r, c: pl.BlockSpec((r, c), lambda i, bb: (0, 0))
    layer_w = lambda r, c: pl.BlockSpec((None, r, c), lambda i, bb: (layer, 0, 0))
    in_specs = _token_specs(streams, step_rows, 0, ctx // ROW_TILE) + [
        pl.BlockSpec((1, 1, d), lambda i, bb: (bb, 0, 0)),
        pl.BlockSpec((1, 1, d), lambda i, bb: (bb, 0, 1)),
        pl.BlockSpec((1, 1, d), lambda i, bb: (batch, 0, 0)),
        pl.BlockSpec((1, 1, d), lambda i, bb: (batch, 0, 1)),
        const2(1, d), layer_w(d, U_COLS), layer_w(256, 512), layer_w(128, 768),
        const2(1, 256), const2(1, 256), const2(1, 256), const2(1, 128),
        tab_spec(256), tab_spec(256), tab_spec(256), tab_spec(256), tab_spec(512), tab_spec(512),
    ]
    qk_spec = lambda n: pl.BlockSpec((1, n, step_rows, GROUP), lambda i, bb: (bb, 0, i, 0))
    qk_shape = lambda n: jax.ShapeDtypeStruct((b, n, t, GROUP), BF16)
    v_shape = jax.ShapeDtypeStruct((b, t, V_ONES_W), BF16)
    return pl.pallas_call(
        functools.partial(_in_kernel, n_ctx_tiles=ctx // ROW_TILE,
                          n_tokens=1 if len(streams) == 1 else 2 * (step_rows // ROW_TILE)),
        grid=(nt, b),
        in_specs=in_specs,
        out_specs=[row_spec(U_SSD_W), qk_spec(1), qk_spec(1), row_spec(V_ONES_W), qk_spec(1), qk_spec(1),
                   row_spec(V_ONES_W), qk_spec(2), qk_spec(2), row_spec(V_ONES_W)],
        out_shape=[jax.ShapeDtypeStruct((b, t, U_SSD_W), F32), qk_shape(1), qk_shape(1), v_shape,
                   qk_shape(1), qk_shape(1), v_shape, qk_shape(2), qk_shape(2), v_shape],
        compiler_params=pltpu.CompilerParams(dimension_semantics=("parallel", "parallel"),
                                             vmem_limit_bytes=VMEM_LIMIT),
        name="in_projection",
    )(*(streams if len(streams) == 1 else streams * (step_rows // ROW_TILE)),
      mod3, mod3, mod3, mod3, g1, w_in, w_uq, w_ukv, gq, gk, mq, mkv, *tables)


def _ssd_kernel(u_ref, cw_ref, cb_ref, dtb_ref, alog_ref, dsk_ref, ng_ref, y_ref,
                xs_s, c_s, yf_s, yb_s, upd_s, win_s, st_s, *, t_rows, ctx):
    L = SSD_CHUNK
    nc = t_rows // L
    ncc = ctx // L
    n_cols = 2 * SSD_HEADS
    row = _iota((L, SSD_CONV_CH), 0)
    lane = _iota((1, LANES), 1)
    a_neg = jnp.where(lane < n_cols, -jnp.exp(alog_ref[...]), 0.0)
    r128 = _iota((L, L), 0)
    c128 = _iota((L, L), 1)
    lane_l = _iota((L, LANES), 1)
    lane256 = _iota((L, GROUP), 1)
    causal = (r128 >= c128, r128 <= c128)
    tri = jnp.where(causal[0], 1.0, 0.0).astype(BF16)
    bd_mask = (_iota((L, GROUP), 0) // SSD_STATE) == (lane256 // (2 * HEAD_DIM))
    group_lanes = [c128 // SSD_STATE == g for g in range(2)]
    head_lanes = [lane256 // HEAD_DIM == h for h in range(SSD_HEADS)]
    sel = [jnp.where(_iota((LANES, GROUP), 0) == d * SSD_HEADS + _iota((LANES, GROUP), 1) // HEAD_DIM,
                     1.0, 0.0).astype(BF16) for d in range(2)]

    def chunk_body(c, carry):
        start = pl.multiple_of(c * L, L)
        x = u_ref[0, pl.ds(start, L), 256:768]
        first = jnp.logical_or(c == 0, c == ncc)
        last = jnp.logical_or(c == ncc - 1, c == nc - 1)
        prev8 = u_ref[0, pl.ds(pl.multiple_of(jnp.maximum(start - 8, 0), 8), 8), 256:768]
        next8 = u_ref[0, pl.ds(pl.multiple_of(jnp.minimum(start + L, t_rows - 8), 8), 8), 256:768]
        prev_row = prev8[7:8, :] * jnp.where(first, 0.0, 1.0)
        next_row = next8[0:1, :] * jnp.where(last, 0.0, 1.0)
        xp = jnp.where(row == 0, prev_row, pltpu.roll(x, 1, axis=0))
        xn = jnp.where(row == L - 1, next_row, pltpu.roll(x, L - 1, axis=0))
        xbc = _silu(cw_ref[0:1, :] * xp + cw_ref[1:2, :] * x + cw_ref[2:3, :] * xn + cb_ref[...])
        xs, bm, cm = xbc[:, 0:256], xbc[:, 256:384], xbc[:, 384:512]
        dtp = _softplus(u_ref[0, pl.ds(start, L), 768:896] + dtb_ref[...])

        a = dtp * a_neg
        cs_f = _dot_sel_lhs(tri, a)
        tot = cs_f[L - 1:L, :]
        cs = jnp.where(lane_l < SSD_HEADS, cs_f, tot - cs_f + a)
        in_dec = jnp.exp(cs)
        w_end = dtp * jnp.exp(tot - cs)
        rows_t = jnp.where(lane_l < n_cols, cs, dtp).T

        xs16 = xs.astype(BF16)
        cb16 = cm.astype(BF16)
        bt16 = bm.T.astype(BF16)
        gram = [_dot(jnp.where(group_lanes[g], cb16, jnp.zeros_like(cb16)), bt16) for g in range(2)]
        x_heads = jnp.concatenate([jnp.where(head_lanes[h], xs16, jnp.zeros_like(xs16))
                                   for h in range(SSD_HEADS)], axis=0)
        for d, y_dst in ((0, yf_s), (1, yb_s)):
            mixes = []
            for h in range(SSD_HEADS):
                j = d * SSD_HEADS + h
                delta = cs[:, j:j + 1] - rows_t[j:j + 1, :]
                decay = jnp.exp(jnp.where(causal[d], delta, NEG_BIG))
                mixes.append((gram[h // 2] * decay * rows_t[n_cols + j:n_cols + j + 1, :]).astype(BF16))
            y_dst[pl.ds(start, L), :] = _dot(jnp.concatenate(mixes, axis=1), x_heads)
            win_s[d, pl.ds(start, L), :] = _dot_sel_rhs2(in_dec, sel[d])
            x_st = (xs * _dot_sel_rhs2(w_end, sel[d])).astype(BF16)
            upd_s[d, pl.ds(start, L), :] = jnp.where(bd_mask, _dot(bt16, x_st), 0.0)
        xs_s[pl.ds(start, L), :] = xs
        c_s[pl.ds(start, L), :] = cb16
        return carry

    _loop_unrolled(nc, chunk_body)

    st_s[...] = jnp.zeros_like(st_s)

    def carry_state(c, d, y_dst):
        start = pl.multiple_of(c * L, L)
        state = st_s[d]
        w_in = win_s[d, pl.ds(start, L), :]
        y_dst[pl.ds(start, L), :] += _dot(c_s[pl.ds(start, L), :], state.astype(BF16)) * w_in
        c_dec = w_in[L - 1:L, :] if d == 0 else w_in[0:1, :]
        st_s[d] = state * c_dec + upd_s[d, pl.ds(start, L), :]

    def scan_body(i, carry):
        carry_state(i, 0, yf_s)
        carry_state(jnp.where(i < ncc, ncc - 1 - i, nc - 1 - (i - ncc)), 1, yb_s)
        return carry

    _loop_unrolled(nc, scan_body)

    d_sum = dsk_ref[0:1, :] + dsk_ref[1:2, :]

    def finish_body(c, carry):
        start = pl.multiple_of(c * L, L)
        y = yf_s[pl.ds(start, L), :] + yb_s[pl.ds(start, L), :] + d_sum * xs_s[pl.ds(start, L), :]
        gated = y * _silu(u_ref[0, pl.ds(start, L), 0:256])
        y_ref[0, pl.ds(start, L), :] = _rms_rows(gated, ng_ref[...], GROUP).astype(y_ref.dtype)
        return carry

    _loop_unrolled(nc, finish_body)


def _ssd(u_ssd, conv_w, conv_b, dt_bias, a_log, d_skip, norm_g, *, ctx):
    b, t, w = u_ssd.shape
    const2 = lambda r, c: pl.BlockSpec((r, c), lambda bb: (0, 0))
    return pl.pallas_call(
        functools.partial(_ssd_kernel, t_rows=t, ctx=ctx),
        grid=(b,),
        in_specs=[pl.BlockSpec((1, t, w), lambda bb: (bb, 0, 0)),
                  const2(3, SSD_CONV_CH), const2(1, SSD_CONV_CH), const2(1, LANES), const2(1, LANES),
                  const2(2, GROUP), const2(1, GROUP)],
        out_specs=pl.BlockSpec((1, t, GROUP), lambda bb: (bb, 0, 0)),
        out_shape=jax.ShapeDtypeStruct((b, t, GROUP), BF16),
        scratch_shapes=[pltpu.VMEM((t, GROUP), F32), pltpu.VMEM((t, LANES), BF16),
                        pltpu.VMEM((t, GROUP), F32), pltpu.VMEM((t, GROUP), F32),
                        pltpu.VMEM((2, t, GROUP), F32), pltpu.VMEM((2, t, GROUP), F32),
                        pltpu.VMEM((2, SSD_CHUNK, GROUP), F32)],
        compiler_params=pltpu.CompilerParams(dimension_semantics=("parallel",),
                                             vmem_limit_bytes=VMEM_LIMIT),
        name="ssd_mixer",
    )(u_ssd, conv_w, conv_b, dt_bias, a_log, d_skip, norm_g)


class AttnCfg(NamedTuple):
    heads: int
    maps: int
    heads_per_block: int
    head_stride: int
    map_width: int


DIFF_CFG = AttnCfg(heads=4, maps=2, heads_per_block=4, head_stride=64, map_width=DIFF_QK)
GQA_CFG = AttnCfg(heads=4, maps=1, heads_per_block=4, head_stride=64, map_width=HEAD_DIM)
MLA_CFG = AttnCfg(heads=4, maps=1, heads_per_block=2, head_stride=MLA_HEAD_PAD, map_width=MLA_HEAD_PAD)


class _AttnGroup:
    def __init__(self, cfg, diff_scale, refs_in, o_ref, scratch, ctx, with_ctx, tq):
        self.cfg, self.diff_scale, self.ctx, self.tq = cfg, diff_scale, ctx, tq
        self.q_ref, self.k_ref, self.v_ref = refs_in[:3]
        self.o_ref = o_ref
        self.s_scr, self.m_scr = scratch[0:2], scratch[2:4]
        self.out_off = ctx if with_ctx else 0
        assert cfg.heads % 2 == 0
        if diff_scale is not None:
            lam_ref, self.ng_ref = refs_in[3:5]
            lp = lam_ref[...]
            self.lam = (jnp.exp(jnp.sum(lp[0:1] * lp[1:2], axis=-1, keepdims=True))
                        - jnp.exp(jnp.sum(lp[2:3] * lp[3:4], axis=-1, keepdims=True)) + diff_scale[0])

    @staticmethod
    def lane_range(lo, width, shape):
        lanes = _iota(shape, 1)
        return jnp.logical_and(lanes >= lo, lanes < lo + width)

    def scores(self, qb, kb, h):
        cfg, out = self.cfg, []
        for j in range(cfg.maps):
            sel = self.lane_range(h * cfg.head_stride + j * cfg.map_width, cfg.map_width, qb.shape)
            qm = jnp.where(sel, qb, jnp.zeros_like(qb))
            s = lax.dot_general(qm, kb, (((1,), (1,)), ((), ())), preferred_element_type=F32)
            out.append((s, jnp.max(s, axis=-1, keepdims=True)))
        return out

    def head_output(self, sm, v, h):
        lo = 0 if h < self.cfg.heads // 2 else LANES
        ones_lane = LANES - lo
        outs = []
        for s, m in sm:
            o = _dot(jnp.exp2(s - m).astype(BF16), v[:, lo:lo + GROUP])
            outs.append(o * (1.0 / o[:, ones_lane:ones_lane + 1]))
        o = outs[0] if self.cfg.maps == 1 else outs[0] - self.lam * outs[1]
        return jnp.where(self.lane_range(h * HEAD_DIM, HEAD_DIM, o.shape), o, 0.0)

    def issue_scores(self, qi, h, slot):
        blk, h_in = divmod(h, self.cfg.heads_per_block)
        row0 = pl.multiple_of(self.ctx + qi * self.tq, self.tq)
        sm = self.scores(self.q_ref[0, blk, pl.ds(row0, self.tq), :], self.k_ref[0, blk], h_in)
        for j, (s, m) in enumerate(sm):
            self.s_scr[slot][j] = s
            self.m_scr[slot][j] = m

    def slot_output(self, h):
        slot = h % 2
        sm = [(self.s_scr[slot][j], self.m_scr[slot][j]) for j in range(self.cfg.maps)]
        return self.head_output(sm, self.v_ref[0], h)

    def store(self, rows, acc):
        if self.diff_scale is not None:
            acc = _rms_segments(acc, self.ng_ref[...], HEAD_DIM) * self.diff_scale[1]
        self.o_ref[0, rows, :] = acc.astype(self.o_ref.dtype)

    def context_block(self):
        ctx = self.ctx
        acc = jnp.zeros((ctx, GROUP), F32)
        for h in range(self.cfg.heads):
            blk, h_in = divmod(h, self.cfg.heads_per_block)
            sm = self.scores(self.q_ref[0, blk, 0:ctx, :], self.k_ref[0, blk, 0:ctx, :], h_in)
            acc = acc + self.head_output(sm, self.v_ref[0, 0:ctx, :], h)
        self.store(slice(0, ctx), acc)


def _attn_kernel(*refs, groups, ctx, with_ctx, tq):
    n_in = [3 if ds is None else 5 for _, ds in groups]
    ins = [refs[sum(n_in[:i]):sum(n_in[:i + 1])] for i in range(len(groups))]
    outs = refs[sum(n_in):sum(n_in) + len(groups)]
    scr = refs[sum(n_in) + len(groups):]
    gs = [_AttnGroup(cfg, ds, ins[i], outs[i], scr[4 * i:4 * i + 4], ctx, with_ctx, tq)
          for i, (cfg, ds) in enumerate(groups)]
    n_heads = groups[0][0].heads
    assert all(cfg.heads == n_heads for cfg, _ in groups)
    n_blocks = (gs[0].k_ref.shape[2] - ctx) // tq

    def block_body(qi, carry, last=False):
        accs = [jnp.zeros((tq, GROUP), F32) for _ in gs]
        for h in range(n_heads):
            for g in gs:
                if h + 1 < n_heads:
                    g.issue_scores(qi, h + 1, (h + 1) % 2)
                elif not last:
                    g.issue_scores(qi + 1, 0, 0)
            accs = [acc + g.slot_output(h) for g, acc in zip(gs, accs)]
        for g, acc in zip(gs, accs):
            g.store(pl.ds(pl.multiple_of(g.out_off + qi * tq, tq), tq), acc)
        return carry

    for g in gs:
        g.issue_scores(0, 0, 0)
    lax.fori_loop(0, n_blocks - 1, block_body, 0)
    block_body(n_blocks - 1, 0, last=True)
    if with_ctx:
        for g in gs:
            g.context_block()


def _attention(inputs, *, groups, ctx, with_ctx, name):
    b, _, t, _ = inputs[0][0].shape
    tq = ATTN_Q_TILE
    assert (t - ctx) % tq == 0
    out_rows = t if with_ctx else t - ctx
    in_specs, args, scratch = [], [], []
    for (q, k, v, *extra), (cfg, _) in zip(inputs, groups):
        qk_spec = pl.BlockSpec((1, q.shape[1], t, GROUP), lambda bb: (bb, 0, 0, 0))
        in_specs += [qk_spec, qk_spec, pl.BlockSpec((1, t, v.shape[-1]), lambda bb: (bb, 0, 0))]
        in_specs += [pl.BlockSpec(e.shape, lambda bb: (0, 0)) for e in extra]
        args += [q, k, v, *extra]
        scratch += [pltpu.VMEM((cfg.maps, tq, t), F32)] * 2 + [pltpu.VMEM((cfg.maps, tq, 1), F32)] * 2
    return pl.pallas_call(
        functools.partial(_attn_kernel, groups=groups, ctx=ctx, with_ctx=with_ctx, tq=tq),
        grid=(b,),
        in_specs=in_specs,
        out_specs=[pl.BlockSpec((1, out_rows, GROUP), lambda bb: (bb, 0, 0))] * len(groups),
        out_shape=[jax.ShapeDtypeStruct((b, out_rows, GROUP), BF16)] * len(groups),
        scratch_shapes=scratch,
        compiler_params=pltpu.CompilerParams(dimension_semantics=("parallel",),
                                             vmem_limit_bytes=VMEM_LIMIT),
        name=name,
    )(*args)


def _stream_weight(src, layer, dst, stage, sem):
    chunk = stage.shape[1]
    n_chunks = dst.shape[0] // chunk

    def copy(i):
        return pltpu.make_async_copy(src.at[layer, pl.ds(i * chunk, chunk), :], stage.at[i % 2], sem.at[i % 2])

    copy(0).start()
    for i in range(n_chunks):
        if i + 1 < n_chunks:
            copy(i + 1).start()
        copy(i).wait()
        dst[i * chunk:(i + 1) * chunk, :] = stage[i % 2].astype(BF16)


def _out_kernel(*refs, layer, final, first_tile, n_ctx_tiles, n_tokens):
    tok_refs = refs[:n_tokens]
    (ya_ref, yb_ref, yc_ref, yd_ref, modb_ref, modc_ref, n2_ref, wo_hbm, w1_hbm, w2_hbm, fg_ref, o_ref,
     wo_ref, w1_ref, w2_ref, stage_wide, stage_tall, sem_wide, sem_tall) = refs[n_tokens:]

    @pl.when(jnp.logical_and(pl.program_id(0) == 0, pl.program_id(1) == 0))
    def _():
        _stream_weight(wo_hbm, layer, wo_ref, stage_tall, sem_tall)
        _stream_weight(w1_hbm, layer, w1_ref, stage_wide, sem_wide)
        _stream_weight(w2_hbm, layer, w2_ref, stage_tall, sem_tall)

    d_model = o_ref.shape[-1]
    n_sub = o_ref.shape[1] // ROW_TILE
    for k in range(n_sub):
        rows = slice(k * ROW_TILE, (k + 1) * ROW_TILE)
        tile = first_tile + pl.program_id(0) * n_sub + k
        is_ctx = jnp.where(tile < n_ctx_tiles, 1.0, 0.0)
        mod = is_ctx * modc_ref[0] + (1.0 - is_ctx) * modb_ref[0]
        gate1, shift2, scale2, gate2 = (mod[:, i * d_model:(i + 1) * d_model] for i in range(2, 6))
        ycat = jnp.concatenate([ya_ref[0, rows, :], yb_ref[0, rows, :], yc_ref[0, rows, :], yd_ref[0, rows, :]],
                               axis=-1)
        h1 = _token_tile(tok_refs, k, tile, n_ctx_tiles) + gate1 * _dot(ycat, wo_ref[...])
        t = _rms_rows(h1, n2_ref[...], d_model) * (1.0 + scale2) + shift2
        mid = jnp.maximum(_dot(t.astype(BF16), w1_ref[...]), 0.0)
        h2 = h1 + gate2 * _dot((mid * mid).astype(BF16), w2_ref[...])
        if final:
            h2 = _rms_rows(h2, fg_ref[...], d_model)
        o_ref[0, rows, :] = h2


def _out_block(streams, ys, mod3, n2, w_out, w1, w2, fg, *, layer, ctx, batch, final):
    b, d = streams[0].shape[0], streams[0].shape[-1]
    t = sum(a.shape[1] for a in streams)
    d_ff = w1.shape[-1]
    out_rows = t - ctx if final else t
    step_rows = next(r for r in (3 * ROW_TILE, 2 * ROW_TILE, ROW_TILE)
                     if out_rows % r == 0 and (not final or ctx % r == 0))
    off = ctx // step_rows if final else 0
    nt = out_rows // step_rows

    row_spec = lambda w, o: pl.BlockSpec((1, step_rows, w), lambda i, bb: (bb, i + o, 0))
    mod_spec = lambda r: pl.BlockSpec((1, 1, 6 * d), lambda i, bb: (bb if r is None else r, 0, 0))
    const2 = lambda r, c: pl.BlockSpec((r, c), lambda i, bb: (0, 0), pipeline_mode=pl.Buffered(1))
    in_hbm = pl.BlockSpec(memory_space=pl.ANY)
    return pl.pallas_call(
        functools.partial(_out_kernel, layer=layer, final=final, first_tile=off * (step_rows // ROW_TILE),
                          n_ctx_tiles=ctx // ROW_TILE,
                          n_tokens=1 if len(streams) == 1 else 2 * (step_rows // ROW_TILE)),
        grid=(nt, b),
        in_specs=_token_specs(streams, step_rows, off * (step_rows // ROW_TILE), ctx // ROW_TILE)
                 + [row_spec(GROUP, off)] + [row_spec(GROUP, 0)] * 3
                 + [mod_spec(None), mod_spec(batch)]
                 + [const2(1, d), in_hbm, in_hbm, in_hbm, const2(1, d)],
        out_specs=pl.BlockSpec((1, step_rows, d), lambda i, bb: (bb, i, 0)),
        out_shape=jax.ShapeDtypeStruct((b, out_rows, d), F32),
        scratch_shapes=[pltpu.VMEM((d, d), BF16), pltpu.VMEM((d, d_ff), BF16), pltpu.VMEM((d_ff, d), BF16),
                        pltpu.VMEM((2, WEIGHT_CHUNK_BYTES // (4 * d_ff), d_ff), F32),
                        pltpu.VMEM((2, WEIGHT_CHUNK_BYTES // (4 * d), d), F32),
                        pltpu.SemaphoreType.DMA((2,)), pltpu.SemaphoreType.DMA((2,))],
        compiler_params=pltpu.CompilerParams(dimension_semantics=("arbitrary", "arbitrary"),
                                             vmem_limit_bytes=VMEM_LIMIT),
        name="out_block",
    )(*(streams if len(streams) == 1 else streams * (step_rows // ROW_TILE)),
      *ys, mod3, mod3, n2, w_out, w1, w2, fg)


def kernel(x, c, ctx, c_ctx, mod_w, mod_b, norm1_g, norm2_g, w_in, ssd_conv_w, ssd_conv_b, ssd_dt_bias,
           ssd_a_log, ssd_d, ssd_norm_g, diff_lambda, diff_norm_g, gqa_q_norm, gqa_k_norm, mla_q_norm,
           mla_kv_norm, mla_w_uq, mla_w_ukv, w_out, mlp_w1, mlp_w2, final_norm_g):
    batch, seq, d_model = x.shape
    n_ctx = ctx.shape[1]
    depth = mod_w.shape[0]
    assert n_ctx % ROW_TILE == 0 and seq % ROW_TILE == 0 and seq % GRID_W == 0
    assert w_in.shape[-1] == 2408 and d_model == 4 * GROUP

    pad_rows = -(batch + 1) % 8
    cond_rows = jnp.concatenate([c, c_ctx[None, :], jnp.zeros((pad_rows, d_model), F32)], axis=0)
    mod_all = _modulation(cond_rows, mod_w, mod_b)

    w_in_p = _take_cols(w_in.astype(BF16), _in_col_sources())
    w_uq_p = jnp.pad(_take_cols(mla_w_uq.astype(BF16), _mla_uq_sources()), ((0, 0), (0, 256 - MLA_Q_LORA), (0, 0)))
    w_ukv_p = _take_cols(mla_w_ukv.astype(BF16), _mla_ukv_sources())
    w_out_b, w1_b, w2_b = w_out, mlp_w1, mlp_w2

    slots32 = [s * 32 for s in range(8)]
    slots64 = [s * 64 for s in range(4)]
    slots_mla = [blk * 256 + hh * MLA_HEAD_PAD + MLA_NOPE for blk in range(2) for hh in range(2)]
    tables = (*_rope_tables(seq, n_ctx, DIFF_QK, slots32, 256),
              *_rope_tables(seq, n_ctx, HEAD_DIM, slots64, 256),
              *_rope_tables(seq, n_ctx, MLA_ROPE, slots_mla, 512))

    tile4 = lambda g: jnp.tile(g, 4)[None, :]
    streams = (ctx, x)
    for i in range(depth):
        final = i == depth - 1
        with_ctx = not final
        lam_init = 0.8 - 0.6 * math.exp(-0.3 * i)
        mod3 = mod_all[i].reshape(mod_all.shape[1], 1, 6 * d_model)
        mq = jnp.pad(mla_q_norm[i], (0, 256 - MLA_Q_LORA))[None, :]
        (u_ssd, qd, kd, vd, qg, kg, vg, qm, km, vm) = _in_projection(
            streams, mod3, norm1_g[i][None, :], w_in_p, w_uq_p, w_ukv_p,
            tile4(gqa_q_norm[i]), tile4(gqa_k_norm[i]), mq, mla_kv_norm[i][None, :], tables,
            layer=i, ctx=n_ctx, batch=batch)

        pad8 = lambda v: jnp.pad(v.reshape(-1), (0, LANES - 2 * SSD_HEADS))[None, :]
        dt_bias2 = jnp.pad(jnp.tile(ssd_dt_bias[i].reshape(-1), 2), (0, LANES - 4 * SSD_HEADS))[None, :]
        y_a = _ssd(u_ssd, ssd_conv_w[i].T, ssd_conv_b[i][None, :], dt_bias2, pad8(ssd_a_log[i]),
                   jnp.repeat(ssd_d[i], HEAD_DIM, axis=-1), ssd_norm_g[i][None, :], ctx=n_ctx)
        (y_b,) = _attention([(qd, kd, vd, diff_lambda[i], tile4(diff_norm_g[i]))],
                            groups=((DIFF_CFG, (lam_init, 1.0 - lam_init)),),
                            ctx=n_ctx, with_ctx=with_ctx, name="diff_attention")
        y_c, y_d = _attention([(qg, kg, vg), (qm, km, vm)], groups=((GQA_CFG, None), (MLA_CFG, None)),
                              ctx=n_ctx, with_ctx=with_ctx, name="softmax_attention")
        streams = (_out_block(streams, (y_a, y_b, y_c, y_d), mod3, norm2_g[i][None, :], w_out_b, w1_b, w2_b,
                              final_norm_g[None, :], layer=i, ctx=n_ctx, batch=batch, final=final),)
    return streams[0]
```
